```python
import math
import jax, jax.numpy as jnp
from jax import lax
import numpy as np

D_MODEL = 2048
BATCH = 2
SEQ = 16384
DEPTH = 4

MIX_WIDTH = D_MODEL
POOL_WIDTH = MIX_WIDTH // 2
POOL_WINDOWS = (2, 4, 8, 16)
N_POOL_GROUPS = len(POOL_WINDOWS)
POOL_GROUP = POOL_WIDTH // N_POOL_GROUPS
DIFF_WIDTH = MIX_WIDTH - POOL_WIDTH
DIFF_HEAD_DIM = 64
DIFF_V_DIM = 2 * DIFF_HEAD_DIM
N_DIFF_HEADS = DIFF_WIDTH // DIFF_V_DIM
QK_WIDTH = N_DIFF_HEADS * 2 * DIFF_HEAD_DIM
IN_WIDTH = POOL_WIDTH + 2 * QK_WIDTH + DIFF_WIDTH
D_FF = 5632
ROPE_THETA = 10000.0
Q_BLOCK = 128
RMS_EPS = 1e-6
SUBLN_EPS = 1e-5
N_MOD = 9

kernel_name = "hybrid_pool_diffattn_macaron_adaln"


def rmsnorm(x, g, eps=RMS_EPS):
    xf = x.astype(jnp.float32)
    y = xf * lax.rsqrt(jnp.mean(xf * xf, axis=-1, keepdims=True) + eps)
    return (y * g.astype(jnp.float32)).astype(x.dtype)


def modulate(h, shift, scale):
    return h * (1 + scale) + shift


def swiglu(h, w_in, w_out):
    a = h @ w_in
    gate, up = a[..., :D_FF], a[..., D_FF:]
    return (jax.nn.silu(gate) * up) @ w_out


def rope_tables(seq_len, dim):
    pos = jnp.arange(seq_len, dtype=jnp.float32)
    inv_freq = ROPE_THETA ** (-jnp.arange(0, dim, 2, dtype=jnp.float32) / dim)
    ang = pos[:, None] * inv_freq[None, :]
    return jnp.cos(ang), jnp.sin(ang)


def apply_rope(t, cos, sin):
    half = t.shape[-1] // 2
    c = cos[None, :, None, None, :].astype(t.dtype)
    s = sin[None, :, None, None, :].astype(t.dtype)
    t1, t2 = t[..., :half], t[..., half:]
    return jnp.concatenate([t1 * c - t2 * s, t1 * s + t2 * c], axis=-1)


def pool_mixer(p, w, b, scale):
    B, S, _ = p.shape
    pf = p.astype(jnp.float32)
    cs = jnp.pad(jnp.cumsum(pf, axis=1), ((0, 0), (1, 0), (0, 0)))
    t = jnp.arange(S)
    groups = []
    for gi, win in enumerate(POOL_WINDOWS):
        lo, hi = gi * POOL_GROUP, (gi + 1) * POOL_GROUP
        cs_g = cs[..., lo:hi]
        start = jnp.maximum(t + 1 - win, 0)
        win_sum = cs_g[:, 1:] - cs_g[:, start]
        count = (t + 1 - start).astype(jnp.float32)
        groups.append(win_sum / count[None, :, None] - pf[..., lo:hi])
    pooled = jnp.stack(groups, axis=2).astype(p.dtype)
    mixed = jnp.einsum('bsgc,gcd->bsgd', pooled, w) + b
    return mixed.reshape(B, S, POOL_WIDTH) * scale


def diff_attention(q, k, v, lam, subln_g, lambda_init):
    B, S, H, _, dh = q.shape
    dv = v.shape[-1]
    n_blocks = S // Q_BLOCK
    qb = q.reshape(B, n_blocks, Q_BLOCK, H, 2, dh).transpose(1, 0, 2, 3, 4, 5)
    sm_scale = dh ** -0.5
    key_pos = jnp.arange(S)
    neg = jnp.finfo(jnp.float32).min

    def one_block(args):
        qi, bi = args
        s = jnp.einsum('bqhcd,bkhcd->bhcqk', qi, k).astype(jnp.float32) * sm_scale
        q_pos = bi * Q_BLOCK + jnp.arange(Q_BLOCK)
        mask = key_pos[None, :] <= q_pos[:, None]
        s = jnp.where(mask, s, neg)
        pm = jax.nn.softmax(s, axis=-1)
        a = pm[:, :, 0] - lam * pm[:, :, 1]
        return jnp.einsum('bhqk,bkhd->bqhd', a.astype(v.dtype), v)

    out = lax.map(one_block, (qb, jnp.arange(n_blocks)))
    out = out.transpose(1, 0, 2, 3, 4).reshape(B, S, H, dv)
    out = rmsnorm(out, subln_g, SUBLN_EPS) * (1.0 - lambda_init)
    return out.reshape(B, S, H * dv)


def setup_inputs(seed: int = 0) -> dict:
    key = jax.random.key(seed)
    ks = jax.random.split(key, 24)
    f32 = jnp.float32
    nrm = lambda k, shape, s: jax.random.normal(k, shape, f32) * s
    L, D, F = DEPTH, D_MODEL, D_FF
    return {
        "x": nrm(ks[0], (BATCH, SEQ, D), 1.0),
        "c": nrm(ks[1], (BATCH, D), 1.0),
        "w_mod": nrm(ks[2], (L, D, N_MOD * D), 0.5 * D ** -0.5),
        "b_mod": nrm(ks[3], (L, N_MOD * D), 0.02),
        "norm_ffn1": 1.0 + nrm(ks[4], (L, D), 0.05),
        "ffn1_w_in": nrm(ks[5], (L, D, 2 * F), D ** -0.5),
        "ffn1_w_out": nrm(ks[6], (L, F, D), F ** -0.5),
        "norm_mix": 1.0 + nrm(ks[7], (L, D), 0.05),
        "w_in": nrm(ks[8], (L, D, IN_WIDTH), D ** -0.5),
        "pool_w": nrm(ks[9], (L, N_POOL_GROUPS, POOL_GROUP, POOL_GROUP), POOL_GROUP ** -0.5),
        "pool_b": nrm(ks[10], (L, N_POOL_GROUPS, POOL_GROUP), 0.02),
        "pool_scale": 1.0 + nrm(ks[11], (L, POOL_WIDTH), 0.1),
        "diff_lambda": nrm(ks[12], (L, 4, DIFF_HEAD_DIM), 0.1),
        "diff_subln": 1.0 + nrm(ks[13], (L, DIFF_V_DIM), 0.05),
        "w_out": nrm(ks[14], (L, MIX_WIDTH, D), MIX_WIDTH ** -0.5),
        "norm_ffn2": 1.0 + nrm(ks[15], (L, D), 0.05),
        "ffn2_w_in": nrm(ks[16], (L, D, 2 * F), D ** -0.5),
        "ffn2_w_out": nrm(ks[17], (L, F, D), F ** -0.5),
        "final_norm": 1.0 + nrm(ks[18], (D,), 0.05),
    }


def reference(x, c, w_mod, b_mod, norm_ffn1, ffn1_w_in, ffn1_w_out, norm_mix, w_in,
              pool_w, pool_b, pool_scale, diff_lambda, diff_subln, w_out,
              norm_ffn2, ffn2_w_in, ffn2_w_out, final_norm):
    B, S, D = x.shape
    cos, sin = rope_tables(S, DIFF_HEAD_DIM)
    c_act = jax.nn.silu(c)
    for l in range(DEPTH):
        lambda_init = 0.8 - 0.6 * math.exp(-0.3 * l)
        mod = (c_act @ w_mod[l] + b_mod[l]).reshape(B, N_MOD, 1, D)
        sh1, sc1, g1, sh2, sc2, g2, sh3, sc3, g3 = [mod[:, i] for i in range(N_MOD)]

        h = modulate(rmsnorm(x, norm_ffn1[l]), sh1, sc1)
        x = x + 0.5 * g1 * swiglu(h, ffn1_w_in[l], ffn1_w_out[l])

        h = modulate(rmsnorm(x, norm_mix[l]), sh2, sc2)
        z = h @ w_in[l]
        o1, o2, o3 = POOL_WIDTH, POOL_WIDTH + QK_WIDTH, POOL_WIDTH + 2 * QK_WIDTH
        p_in = z[..., :o1]
        q = apply_rope(z[..., o1:o2].reshape(B, S, N_DIFF_HEADS, 2, DIFF_HEAD_DIM), cos, sin)
        k = apply_rope(z[..., o2:o3].reshape(B, S, N_DIFF_HEADS, 2, DIFF_HEAD_DIM), cos, sin)
        v = z[..., o3:].reshape(B, S, N_DIFF_HEADS, DIFF_V_DIM)

        lam_p = diff_lambda[l].astype(jnp.float32)
        lam = (jnp.exp(jnp.sum(lam_p[0] * lam_p[1])) - jnp.exp(jnp.sum(lam_p[2] * lam_p[3]))
               + lambda_init)

        y_pool = pool_mixer(p_in, pool_w[l], pool_b[l], pool_scale[l])
        y_attn = diff_attention(q, k, v, lam, diff_subln[l], lambda_init)
        y = jnp.concatenate([y_pool, y_attn], axis=-1) @ w_out[l]
        x = x + g2 * y

        h = modulate(rmsnorm(x, norm_ffn2[l]), sh3, sc3)
        x = x + 0.5 * g3 * swiglu(h, ffn2_w_in[l], ffn2_w_out[l])
    return rmsnorm(x, final_norm)
```

```python
import functools
import math

import jax
import jax.numpy as jnp
from jax import lax
from jax.experimental import pallas as pl
from jax.experimental.pallas import tpu as pltpu

F32 = jnp.float32
BF16 = jnp.bfloat16

POOL_WINDOWS = (2, 4, 8, 16)
POOL_HALO = 16
HEAD_DIM = 64
HEAD_V = 128
ROPE_THETA = 10000.0
RMS_EPS = 1e-6
SUBLN_EPS = 1e-5
N_MOD = 9
MASK_VALUE = -1e30

VMEM_LIMIT_BYTES = 56 * 1024 * 1024


def _params(sem):
    return pltpu.CompilerParams(dimension_semantics=sem, vmem_limit_bytes=VMEM_LIMIT_BYTES)


def _resident(block_shape, index_map):
    return pl.BlockSpec(block_shape, index_map, pipeline_mode=pl.Buffered(1))


def _norm_modulate(x, g, shift, scale):
    y = x * lax.rsqrt(jnp.mean(x * x, axis=-1, keepdims=True) + RMS_EPS) * g
    return y * (1.0 + scale) + shift


def _mod_kernel(ct_ref, w_ref, b_ref, o_ref):
    ct = ct_ref[...]
    ct = ct * jax.nn.sigmoid(ct)
    w = w_ref[...]
    rows = [jnp.sum(w * ct[:, b:b + 1], axis=0, keepdims=True) for b in range(ct.shape[1])]
    o_ref[...] = jnp.concatenate(rows, axis=0) + b_ref[...]


def _modulation(c, w_mod, b_mod, tn=1024):
    L, D, N = w_mod.shape
    B = c.shape[0]
    return pl.pallas_call(
        _mod_kernel,
        grid=(L, N // tn),
        in_specs=[
            pl.BlockSpec((D, B), lambda l, n: (0, 0)),
            pl.BlockSpec((None, D, tn), lambda l, n: (l, 0, n)),
            pl.BlockSpec((None, 1, tn), lambda l, n: (l, 0, n)),
        ],
        out_specs=pl.BlockSpec((None, B, tn), lambda l, n: (l, 0, n)),
        out_shape=jax.ShapeDtypeStruct((L, B, N), F32),
        compiler_params=_params(("parallel", "parallel")),
        name="modulation",
    )(c.T, w_mod, b_mod.reshape(L, 1, N))


def _ffn_kernel(x_ref, mod_ref, g_ref, wg_ref, wu_ref, wo_ref, o_ref, h_ref, *, row0):
    j = pl.program_id(1)

    @pl.when(j == 0)
    def _():
        h = _norm_modulate(x_ref[...], g_ref[...], mod_ref[row0:row0 + 1, :], mod_ref[row0 + 1:row0 + 2, :])
        h_ref[...] = h.astype(BF16)
        o_ref[...] = jnp.zeros_like(o_ref)

    h = h_ref[...]
    a_g = jnp.dot(h, wg_ref[...], preferred_element_type=F32)
    a_u = jnp.dot(h, wu_ref[...], preferred_element_type=F32)
    act = (a_g * jax.nn.sigmoid(a_g) * a_u).astype(BF16)
    o_ref[...] += jnp.dot(act, wo_ref[...], preferred_element_type=F32)

    @pl.when(j == pl.num_programs(1) - 1)
    def _():
        o_ref[...] = x_ref[...] + 0.5 * mod_ref[row0 + 2:row0 + 3, :] * o_ref[...]


def _ffn(x, mod, g, w_in, w_out, *, row0, seq, tm, tf):
    T, D = x.shape
    F = w_out.shape[0]
    tpb = seq // tm
    nf = F // tf
    return pl.pallas_call(
        functools.partial(_ffn_kernel, row0=row0),
        grid=(T // tm, nf),
        in_specs=[
            pl.BlockSpec((tm, D), lambda i, j: (i, 0)),
            pl.BlockSpec((None, N_MOD, D), lambda i, j: (i // tpb, 0, 0)),
            pl.BlockSpec((1, D), lambda i, j: (0, 0)),
            pl.BlockSpec((D, tf), lambda i, j: (0, j)),
            pl.BlockSpec((D, tf), lambda i, j: (0, nf + j)),
            pl.BlockSpec((tf, D), lambda i, j: (j, 0)),
        ],
        out_specs=pl.BlockSpec((tm, D), lambda i, j: (i, 0)),
        out_shape=jax.ShapeDtypeStruct((T, D), F32),
        scratch_shapes=[pltpu.VMEM((tm, D), BF16)],
        compiler_params=_params(("parallel", "arbitrary")),
        name="ffn",
    )(x, mod, g, w_in, w_in, w_out)


def _mix_in_kernel(x_ref, mod_ref, g_ref, w_ref, cos_ref, sin_ref, pw_ref, pb_ref, ps_ref,
                   q_ref, k_ref, v_ref, yp_ref, pext_ref, *, tpb, width, n_heads):
    i = pl.program_id(0)
    tm = x_ref.shape[0]
    h = _norm_modulate(x_ref[...], g_ref[...], mod_ref[3:4, :], mod_ref[4:5, :]).astype(BF16)

    p = jnp.dot(h, w_ref[:, 0:width], preferred_element_type=F32)

    @pl.when(i % tpb == 0)
    def _():
        pext_ref[0:POOL_HALO, :] = jnp.zeros((POOL_HALO, width), F32)

    pext_ref[POOL_HALO:POOL_HALO + tm, :] = p
    pos = (i % tpb) * tm + lax.broadcasted_iota(jnp.int32, (tm, 1), 0)
    cg = width // len(POOL_WINDOWS)
    for gi, win in enumerate(POOL_WINDOWS):
        lo, hi = gi * cg, (gi + 1) * cg
        pg = p[:, lo:hi]
        acc = pg
        for back in range(1, win):
            acc = acc + pext_ref[POOL_HALO - back:POOL_HALO - back + tm, lo:hi]
        count = jnp.minimum(pos + 1, win).astype(F32)
        pooled = acc / count - pg
        mixed = jnp.dot(pooled.astype(BF16), pw_ref[gi], preferred_element_type=F32) + pb_ref[:, lo:hi]
        yp_ref[:, lo:hi] = (mixed * ps_ref[:, lo:hi]).astype(BF16)
    pext_ref[0:POOL_HALO, :] = pext_ref[tm:tm + POOL_HALO, :]

    cos = cos_ref[...]
    sin = sin_ref[...]
    lane = lax.broadcasted_iota(jnp.int32, (tm, HEAD_V), 1)
    first_half = (lane % HEAD_DIM) < (HEAD_DIM // 2)
    for sec, out_ref, mul in ((1, q_ref, HEAD_DIM ** -0.5), (2, k_ref, None)):
        z = jnp.dot(h, w_ref[:, sec * width:(sec + 1) * width], preferred_element_type=F32)
        for hh in range(n_heads):
            zh = z[:, hh * HEAD_V:(hh + 1) * HEAD_V]
            partner = jnp.where(first_half,
                                pltpu.roll(zh, HEAD_V - HEAD_DIM // 2, axis=1),
                                pltpu.roll(zh, HEAD_DIM // 2, axis=1))
            r = zh * cos + partner * sin
            if mul is not None:
                r = r * mul
            out_ref[hh] = r.astype(BF16)

    z = jnp.dot(h, w_ref[:, 3 * width:4 * width], preferred_element_type=F32)
    for hh in range(n_heads):
        v_ref[hh] = z[:, hh * HEAD_V:(hh + 1) * HEAD_V].astype(BF16)


def _mix_in(x, mod, g, w_in, cos, sin, pool_w, pool_b, pool_scale, *, seq, tm):
    T, D = x.shape
    width = w_in.shape[1] // 4
    n_heads = width // HEAD_V
    tpb = seq // tm
    head_major = jax.ShapeDtypeStruct((n_heads, T, HEAD_V), BF16)
    head_spec = pl.BlockSpec((n_heads, tm, HEAD_V), lambda i: (0, i, 0))
    return pl.pallas_call(
        functools.partial(_mix_in_kernel, tpb=tpb, width=width, n_heads=n_heads),
        grid=(T // tm,),
        in_specs=[
            pl.BlockSpec((tm, D), lambda i: (i, 0)),
            pl.BlockSpec((None, N_MOD, D), lambda i: (i // tpb, 0, 0)),
            pl.BlockSpec((1, D), lambda i: (0, 0)),
            _resident((D, 4 * width), lambda i: (0, 0)),
            pl.BlockSpec((tm, HEAD_V), lambda i: (i % tpb, 0)),
            pl.BlockSpec((tm, HEAD_V), lambda i: (i % tpb, 0)),
            _resident(pool_w.shape, lambda i: (0, 0, 0)),
            pl.BlockSpec((1, width), lambda i: (0, 0)),
            pl.BlockSpec((1, width), lambda i: (0, 0)),
        ],
        out_specs=[head_spec, head_spec, head_spec, pl.BlockSpec((tm, width), lambda i: (i, 0))],
        out_shape=[head_major, head_major, head_major, jax.ShapeDtypeStruct((T, width), BF16)],
        scratch_shapes=[pltpu.VMEM((POOL_HALO + tm, width), F32)],
        compiler_params=_params(("arbitrary",)),
        name="mix_in",
    )(x, mod, g, w_in, cos, sin, pool_w, pool_b, pool_scale)


def _attn_kernel(linit_ref, q_ref, k_ref, v_ref, dl_ref, g_ref, o_ref, qs_ref, m_ref, l_ref, acc_ref):
    i = pl.program_id(2)
    tq = q_ref.shape[0]
    tk = tq

    q = q_ref[...]
    lane = lax.broadcasted_iota(jnp.int32, q.shape, 1)
    qs_ref[0:tq, :] = jnp.where(lane < HEAD_DIM, q, jnp.zeros_like(q))
    qs_ref[tq:2 * tq, :] = jnp.where(lane < HEAD_DIM, jnp.zeros_like(q), q)
    m_ref[...] = jnp.full(m_ref.shape, MASK_VALUE, F32)
    l_ref[...] = jnp.zeros(l_ref.shape, F32)
    acc_ref[...] = jnp.zeros(acc_ref.shape, F32)

    def sweep(j, diagonal):
        start = pl.multiple_of(j * tk, tk)
        kj = k_ref[pl.ds(start, tk), :]
        vj = v_ref[pl.ds(start, tk), :]
        s = lax.dot_general(qs_ref[...], kj, (((1,), (1,)), ((), ())), preferred_element_type=F32)
        if diagonal:
            row = lax.broadcasted_iota(jnp.int32, (2 * tq, tk), 0)
            col = lax.broadcasted_iota(jnp.int32, (2 * tq, tk), 1)
            row = jnp.where(row >= tq, row - tq, row)
            s = jnp.where(col <= row, s, MASK_VALUE)
        m_prev = m_ref[...]
        m_new = jnp.maximum(m_prev, jnp.max(s, axis=1, keepdims=True))
        alpha = jnp.exp(m_prev - m_new)
        p = jnp.exp(s - m_new)
        l_ref[...] = alpha * l_ref[...] + jnp.sum(p, axis=1, keepdims=True)
        acc_ref[...] = alpha * acc_ref[...] + jnp.dot(p.astype(BF16), vj, preferred_element_type=F32)
        m_ref[...] = m_new

    def body(j, carry):
        sweep(j, False)
        return carry

    lax.fori_loop(0, i, body, 0)
    sweep(i, True)

    dl = dl_ref[...]
    lambda_init = linit_ref[0]
    lam = (jnp.exp(jnp.sum(dl[0:1] * dl[1:2], axis=1, keepdims=True))
           - jnp.exp(jnp.sum(dl[2:3] * dl[3:4], axis=1, keepdims=True)) + lambda_init)
    o = acc_ref[0:tq, :] / l_ref[0:tq, :] - lam * (acc_ref[tq:2 * tq, :] / l_ref[tq:2 * tq, :])
    y = o * lax.rsqrt(jnp.mean(o * o, axis=-1, keepdims=True) + SUBLN_EPS) * g_ref[...]
    o_ref[...] = (y * (1.0 - lambda_init)).astype(BF16)


def _attention(linit, q, k, v, diff_lambda, subln, *, batch, seq, tq):
    n_heads, T, _ = q.shape
    nq = seq // tq
    return pl.pallas_call(
        _attn_kernel,
        grid=(batch, n_heads, nq),
        in_specs=[
            pl.BlockSpec(memory_space=pltpu.SMEM),
            pl.BlockSpec((None, tq, HEAD_V), lambda b, h, i: (h, b * nq + i, 0)),
            pl.BlockSpec((None, seq, HEAD_V), lambda b, h, i: (h, b, 0)),
            pl.BlockSpec((None, seq, HEAD_V), lambda b, h, i: (h, b, 0)),
            pl.BlockSpec(diff_lambda.shape, lambda b, h, i: (0, 0)),
            pl.BlockSpec((1, HEAD_V), lambda b, h, i: (0, 0)),
        ],
        out_specs=pl.BlockSpec((tq, HEAD_V), lambda b, h, i: (b * nq + i, h)),
        out_shape=jax.ShapeDtypeStruct((T, n_heads * HEAD_V), BF16),
        scratch_shapes=[
            pltpu.VMEM((2 * tq, HEAD_V), BF16),
            pltpu.VMEM((2 * tq, 1), F32),
            pltpu.VMEM((2 * tq, 1), F32),
            pltpu.VMEM((2 * tq, HEAD_V), F32),
        ],
        compiler_params=_params(("parallel", "parallel", "arbitrary")),
        name="diff_attn",
    )(linit, q, k, v, diff_lambda, subln)


def _mix_out_kernel(x_ref, mod_ref, yp_ref, ya_ref, w_ref, o_ref):
    width = yp_ref.shape[1]
    y = jnp.dot(yp_ref[...], w_ref[0:width, :], preferred_element_type=F32)
    y = y + jnp.dot(ya_ref[...], w_ref[width:2 * width, :], preferred_element_type=F32)
    o_ref[...] = x_ref[...] + mod_ref[5:6, :] * y


def _mix_out(x, mod, y_pool, y_attn, w_out, *, seq, tm):
    T, D = x.shape
    width = y_pool.shape[1]
    tpb = seq // tm
    return pl.pallas_call(
        _mix_out_kernel,
        grid=(T // tm,),
        in_specs=[
            pl.BlockSpec((tm, D), lambda i: (i, 0)),
            pl.BlockSpec((None, N_MOD, D), lambda i: (i // tpb, 0, 0)),
            pl.BlockSpec((tm, width), lambda i: (i, 0)),
            pl.BlockSpec((tm, width), lambda i: (i, 0)),
            _resident(w_out.shape, lambda i: (0, 0)),
        ],
        out_specs=pl.BlockSpec((tm, D), lambda i: (i, 0)),
        out_shape=jax.ShapeDtypeStruct((T, D), F32),
        compiler_params=_params(("parallel",)),
        name="mix_out",
    )(x, mod, y_pool, y_attn, w_out)


def _final_norm_kernel(x_ref, g_ref, o_ref):
    x = x_ref[...]
    o_ref[...] = x * lax.rsqrt(jnp.mean(x * x, axis=-1, keepdims=True) + RMS_EPS) * g_ref[...]


def _final_norm(x, g, *, tm):
    T, D = x.shape
    return pl.pallas_call(
        _final_norm_kernel,
        grid=(T // tm,),
        in_specs=[pl.BlockSpec((tm, D), lambda i: (i, 0)), pl.BlockSpec((1, D), lambda i: (0, 0))],
        out_specs=pl.BlockSpec((tm, D), lambda i: (i, 0)),
        out_shape=jax.ShapeDtypeStruct((T, D), F32),
        compiler_params=_params(("parallel",)),
        name="final_norm",
    )(x, g)


def _rope_tables(seq):
    pos = jnp.arange(seq, dtype=F32)
    inv_freq = ROPE_THETA ** (-jnp.arange(0, HEAD_DIM, 2, dtype=F32) / HEAD_DIM)
    ang = pos[:, None] * inv_freq[None, :]
    c, s = jnp.cos(ang), jnp.sin(ang)
    return jnp.concatenate([c, c, c, c], axis=1), jnp.concatenate([-s, s, -s, s], axis=1)


def kernel(x, c, w_mod, b_mod, norm_ffn1, ffn1_w_in, ffn1_w_out, norm_mix, w_in, pool_w, pool_b, pool_scale,
           diff_lambda, diff_subln, w_out, norm_ffn2, ffn2_w_in, ffn2_w_out, final_norm):
    B, S, D = x.shape
    L = w_mod.shape[0]
    F = ffn1_w_out.shape[1]
    tm = min(512, S)
    tq = min(256, S)
    tf = 512 if F % 512 == 0 else F
    width = pool_scale.shape[1]

    cos, sin = _rope_tables(S)
    mod_all = _modulation(c, w_mod, b_mod).reshape(L, B, N_MOD, D)
    xt = x.reshape(B * S, D)
    for l in range(L):
        mod = mod_all[l]
        linit = jnp.full((1,), 0.8 - 0.6 * math.exp(-0.3 * l), F32)
        xt = _ffn(xt, mod, norm_ffn1[l].reshape(1, D), ffn1_w_in[l].astype(BF16), ffn1_w_out[l].astype(BF16),
                  row0=0, seq=S, tm=tm, tf=tf)
        q, k, v, y_pool = _mix_in(xt, mod, norm_mix[l].reshape(1, D), w_in[l].astype(BF16), cos, sin,
                                  pool_w[l].astype(BF16), pool_b[l].reshape(1, width),
                                  pool_scale[l].reshape(1, width), seq=S, tm=tm)
        y_attn = _attention(linit, q, k, v, diff_lambda[l], diff_subln[l].reshape(1, HEAD_V),
                            batch=B, seq=S, tq=tq)
        xt = _mix_out(xt, mod, y_pool, y_attn, w_out[l].astype(BF16), seq=S, tm=tm)
        xt = _ffn(xt, mod, norm_ffn2[l].reshape(1, D), ffn2_w_in[l].astype(BF16), ffn2_w_out[l].astype(BF16),
                  row0=6, seq=S, tm=tm, tf=tf)
    return _final_norm(xt, final_norm.reshape(1, D), tm=tm).reshape(B, S, D)
```

```python
import functools
import math

import jax
import jax.numpy as jnp
from jax import lax
from jax.experimental import pallas as pl
from jax.experimental.pallas import tpu as pltpu

F32 = jnp.float32
BF16 = jnp.bfloat16

POOL_WINDOWS = (2, 4, 8, 16)
POOL_HALO = 16
HEAD_DIM = 64
HEAD_V = 128
ROPE_THETA = 10000.0
RMS_EPS = 1e-6
SUBLN_EPS = 1e-5
N_MOD = 9
MASK_VALUE = -1e30

VMEM_LIMIT_BYTES = 56 * 1024 * 1024

TOKEN_TILE = 512
FF_CHUNK = 512
Q_TILE = 256
KV_CHUNK = 512


def _params(sem):
    return pltpu.CompilerParams(dimension_semantics=sem, vmem_limit_bytes=VMEM_LIMIT_BYTES)


def _resident(block_shape, index_map):
    return pl.BlockSpec(block_shape, index_map, pipeline_mode=pl.Buffered(1))


def _norm_modulate(x, g, shift, scale):
    y = x * lax.rsqrt(jnp.mean(x * x, axis=-1, keepdims=True) + RMS_EPS) * g
    return y * (1.0 + scale) + shift


def _mod_kernel(ct_ref, w_ref, b_ref, o_ref):
    ct = ct_ref[...]
    ct = ct * jax.nn.sigmoid(ct)
    w = w_ref[...]
    rows = [jnp.sum(w * ct[:, b:b + 1], axis=0, keepdims=True) for b in range(ct.shape[1])]
    o_ref[...] = jnp.concatenate(rows, axis=0) + b_ref[...]


def _modulation(c, w_mod, b_mod, tn=1024):
    L, D, N = w_mod.shape
    B = c.shape[0]
    return pl.pallas_call(
        _mod_kernel,
        grid=(L, N // tn),
        in_specs=[
            pl.BlockSpec((D, B), lambda l, n: (0, 0)),
            pl.BlockSpec((None, D, tn), lambda l, n: (l, 0, n)),
            pl.BlockSpec((None, 1, tn), lambda l, n: (l, 0, n)),
        ],
        out_specs=pl.BlockSpec((None, B, tn), lambda l, n: (l, 0, n)),
        out_shape=jax.ShapeDtypeStruct((L, B, N), F32),
        compiler_params=_params(("parallel", "parallel")),
        name="modulation",
    )(c.T, w_mod, b_mod.reshape(L, 1, N))


def _ffn_kernel(x_ref, mod_ref, g_ref, wg_ref, wu_ref, wo_ref, o_ref, h_ref, *, row0):
    j = pl.program_id(1)

    @pl.when(j == 0)
    def _():
        h = _norm_modulate(x_ref[...], g_ref[...], mod_ref[row0:row0 + 1, :], mod_ref[row0 + 1:row0 + 2, :])
        h_ref[...] = h.astype(BF16)
        o_ref[...] = jnp.zeros_like(o_ref)

    h = h_ref[...]
    a_g = jnp.dot(h, wg_ref[...], preferred_element_type=F32)
    a_u = jnp.dot(h, wu_ref[...], preferred_element_type=F32)
    act = (a_g * jax.nn.sigmoid(a_g) * a_u).astype(BF16)
    o_ref[...] += jnp.dot(act, wo_ref[...], preferred_element_type=F32)

    @pl.when(j == pl.num_programs(1) - 1)
    def _():
        o_ref[...] = x_ref[...] + 0.5 * mod_ref[row0 + 2:row0 + 3, :] * o_ref[...]


def _ffn(x, mod, g, w_in, w_out, *, row0, seq, tm, tf):
    T, D = x.shape
    F = w_out.shape[0]
    tpb = seq // tm
    nf = F // tf
    return pl.pallas_call(
        functools.partial(_ffn_kernel, row0=row0),
        grid=(T // tm, nf),
        in_specs=[
            pl.BlockSpec((tm, D), lambda i, j: (i, 0)),
            pl.BlockSpec((None, N_MOD, D), lambda i, j: (i // tpb, 0, 0)),
            pl.BlockSpec((1, D), lambda i, j: (0, 0)),
            pl.BlockSpec((D, tf), lambda i, j: (0, j)),
            pl.BlockSpec((D, tf), lambda i, j: (0, nf + j)),
            pl.BlockSpec((tf, D), lambda i, j: (j, 0)),
        ],
        out_specs=pl.BlockSpec((tm, D), lambda i, j: (i, 0)),
        out_shape=jax.ShapeDtypeStruct((T, D), F32),
        scratch_shapes=[pltpu.VMEM((tm, D), BF16)],
        compiler_params=_params(("parallel", "arbitrary")),
        name="ffn",
    )(x, mod, g, w_in, w_in, w_out)


def _mix_in_kernel(x_ref, mod_ref, g_ref, w_ref, wvt_ref, cos_ref, sin_ref, pw_ref, pb_ref, ps_ref,
                   q_ref, k_ref, vt_ref, yp_ref, pext_ref, *, tpb, width, n_heads):
    i = pl.program_id(0)
    tm = x_ref.shape[0]
    h = _norm_modulate(x_ref[...], g_ref[...], mod_ref[3:4, :], mod_ref[4:5, :]).astype(BF16)

    p = jnp.dot(h, w_ref[:, 0:width], preferred_element_type=F32)

    @pl.when(i % tpb == 0)
    def _():
        pext_ref[0:POOL_HALO, :] = jnp.zeros((POOL_HALO, width), F32)

    pext_ref[POOL_HALO:POOL_HALO + tm, :] = p
    pos = (i % tpb) * tm + lax.broadcasted_iota(jnp.int32, (tm, 1), 0)
    cg = width // len(POOL_WINDOWS)
    for gi, win in enumerate(POOL_WINDOWS):
        lo, hi = gi * cg, (gi + 1) * cg
        pg = p[:, lo:hi]
        acc = pg
        for back in range(1, win):
            acc = acc + pext_ref[POOL_HALO - back:POOL_HALO - back + tm, lo:hi]
        count = jnp.minimum(pos + 1, win).astype(F32)
        pooled = acc / count - pg
        mixed = jnp.dot(pooled.astype(BF16), pw_ref[gi], preferred_element_type=F32) + pb_ref[:, lo:hi]
        yp_ref[:, lo:hi] = (mixed * ps_ref[:, lo:hi]).astype(BF16)
    pext_ref[0:POOL_HALO, :] = pext_ref[tm:tm + POOL_HALO, :]

    cos = cos_ref[...]
    sin = sin_ref[...]
    lane = lax.broadcasted_iota(jnp.int32, (tm, HEAD_V), 1)
    first_half = (lane % HEAD_DIM) < (HEAD_DIM // 2)
    for sec, out_ref, mul in ((1, q_ref, HEAD_DIM ** -0.5), (2, k_ref, None)):
        z = jnp.dot(h, w_ref[:, sec * width:(sec + 1) * width], preferred_element_type=F32)
        for hh in range(n_heads):
            zh = z[:, hh * HEAD_V:(hh + 1) * HEAD_V]
            partner = jnp.where(first_half,
                                pltpu.roll(zh, HEAD_V - HEAD_DIM // 2, axis=1),
                                pltpu.roll(zh, HEAD_DIM // 2, axis=1))
            r = zh * cos + partner * sin
            if mul is not None:
                r = r * mul
            out_ref[hh] = r.astype(BF16)

    tk = vt_ref.shape[3]
    zt = lax.dot_general(wvt_ref[...], h, (((1,), (1,)), ((), ())), preferred_element_type=F32)
    for hh in range(n_heads):
        for ck in range(tm // tk):
            vt_ref[hh, ck] = zt[hh * HEAD_V:(hh + 1) * HEAD_V, ck * tk:(ck + 1) * tk].astype(BF16)


def _mix_in(x, mod, g, w_in, w_vt, cos, sin, pool_w, pool_b, pool_scale, *, seq, tm, tk):
    T, D = x.shape
    width = w_vt.shape[0]
    n_heads = width // HEAD_V
    tpb = seq // tm
    head_major = jax.ShapeDtypeStruct((n_heads, T, HEAD_V), BF16)
    head_spec = pl.BlockSpec((n_heads, tm, HEAD_V), lambda i: (0, i, 0))
    vt_shape = jax.ShapeDtypeStruct((n_heads, T // tk, HEAD_V, tk), BF16)
    vt_spec = pl.BlockSpec((n_heads, tm // tk, HEAD_V, tk), lambda i: (0, i, 0, 0))
    return pl.pallas_call(
        functools.partial(_mix_in_kernel, tpb=tpb, width=width, n_heads=n_heads),
        grid=(T // tm,),
        in_specs=[
            pl.BlockSpec((tm, D), lambda i: (i, 0)),
            pl.BlockSpec((None, N_MOD, D), lambda i: (i // tpb, 0, 0)),
            pl.BlockSpec((1, D), lambda i: (0, 0)),
            _resident((D, 3 * width), lambda i: (0, 0)),
            _resident((width, D), lambda i: (0, 0)),
            pl.BlockSpec((tm, HEAD_V), lambda i: (i % tpb, 0)),
            pl.BlockSpec((tm, HEAD_V), lambda i: (i % tpb, 0)),
            _resident(pool_w.shape, lambda i: (0, 0, 0)),
            pl.BlockSpec((1, width), lambda i: (0, 0)),
            pl.BlockSpec((1, width), lambda i: (0, 0)),
        ],
        out_specs=[head_spec, head_spec, vt_spec, pl.BlockSpec((tm, width), lambda i: (i, 0))],
        out_shape=[head_major, head_major, vt_shape, jax.ShapeDtypeStruct((T, width), BF16)],
        scratch_shapes=[pltpu.VMEM((POOL_HALO + tm, width), F32)],
        compiler_params=_params(("arbitrary",)),
        name="mix_in",
    )(x, mod, g, w_in, w_vt, cos, sin, pool_w, pool_b, pool_scale)


def _attn_kernel(linit_ref, q_ref, k_ref, vt_ref, dl_ref, g_ref, o_ref,
                 qs_ref, s_ref, mc_ref, m_ref, l_ref, acc_ref):
    i = pl.program_id(2)
    tq = q_ref.shape[0]
    tk = vt_ref.shape[2]

    q = q_ref[...]
    lane = lax.broadcasted_iota(jnp.int32, q.shape, 1)
    qs_ref[0:tq, :] = jnp.where(lane < HEAD_DIM, q, jnp.zeros_like(q))
    qs_ref[tq:2 * tq, :] = jnp.where(lane < HEAD_DIM, jnp.zeros_like(q), q)
    m_ref[...] = jnp.full(m_ref.shape, MASK_VALUE, F32)
    l_ref[...] = jnp.zeros(l_ref.shape, F32)
    acc_ref[...] = jnp.zeros(acc_ref.shape, F32)

    last = (i * tq) // tk

    def scores(j, slot, masked):
        start = pl.multiple_of(j * tk, tk)
        kj = k_ref[pl.ds(start, tk), :]
        s = lax.dot_general(kj, qs_ref[...], (((1,), (1,)), ((), ())), preferred_element_type=F32)
        if masked:
            key_pos = start + lax.broadcasted_iota(jnp.int32, s.shape, 0)
            col = lax.broadcasted_iota(jnp.int32, s.shape, 1)
            q_pos = i * tq + jnp.where(col >= tq, col - tq, col)
            s = jnp.where(key_pos <= q_pos, s, MASK_VALUE)
        s_ref[slot] = s
        mc_ref[slot] = jnp.max(s, axis=0, keepdims=True)

    def accumulate(j, slot):
        m_prev = m_ref[...]
        m_new = jnp.maximum(m_prev, mc_ref[slot])
        alpha = jnp.exp(m_prev - m_new)
        p = jnp.exp(s_ref[slot] - m_new)
        l_ref[...] = alpha * l_ref[...] + jnp.sum(p, axis=0, keepdims=True)
        acc_ref[...] = alpha * acc_ref[...] + jnp.dot(vt_ref[j], p.astype(BF16), preferred_element_type=F32)
        m_ref[...] = m_new

    @pl.when(last == 0)
    def _():
        scores(0, 0, True)

    @pl.when(last > 0)
    def _():
        scores(0, 0, False)

    def body(j, carry):
        accumulate(j, j % 2)
        scores(j + 1, (j + 1) % 2, False)
        return carry

    lax.fori_loop(0, last - 1, body, 0)

    @pl.when(last > 0)
    def _():
        accumulate(last - 1, (last - 1) % 2)
        scores(last, last % 2, True)

    accumulate(last, last % 2)

    dl = dl_ref[...]
    lambda_init = linit_ref[0]
    lam = (jnp.exp(jnp.sum(dl[0:1] * dl[1:2], axis=1, keepdims=True))
           - jnp.exp(jnp.sum(dl[2:3] * dl[3:4], axis=1, keepdims=True)) + lambda_init)
    o = acc_ref[:, 0:tq] / l_ref[:, 0:tq] - lam * (acc_ref[:, tq:2 * tq] / l_ref[:, tq:2 * tq])
    o = o * lax.rsqrt(jnp.mean(o * o, axis=0, keepdims=True) + SUBLN_EPS)
    o_ref[...] = (o.T * g_ref[...] * (1.0 - lambda_init)).astype(BF16)


def _attention(linit, q, k, vt, diff_lambda, subln, *, batch, seq, tq):
    n_heads, T, _ = q.shape
    tk = vt.shape[3]
    nq = seq // tq
    nk = seq // tk
    return pl.pallas_call(
        _attn_kernel,
        grid=(batch, n_heads, nq),
        in_specs=[
            pl.BlockSpec(memory_space=pltpu.SMEM),
            pl.BlockSpec((None, tq, HEAD_V), lambda b, h, i: (h, b * nq + i, 0)),
            pl.BlockSpec((None, seq, HEAD_V), lambda b, h, i: (h, b, 0)),
            pl.BlockSpec((None, nk, HEAD_V, tk), lambda b, h, i: (h, b, 0, 0)),
            pl.BlockSpec(diff_lambda.shape, lambda b, h, i: (0, 0)),
            pl.BlockSpec((1, HEAD_V), lambda b, h, i: (0, 0)),
        ],
        out_specs=pl.BlockSpec((tq, HEAD_V), lambda b, h, i: (b * nq + i, h)),
        out_shape=jax.ShapeDtypeStruct((T, n_heads * HEAD_V), BF16),
        scratch_shapes=[
            pltpu.VMEM((2 * tq, HEAD_V), BF16),
            pltpu.VMEM((2, tk, 2 * tq), F32),
            pltpu.VMEM((2, 1, 2 * tq), F32),
            pltpu.VMEM((1, 2 * tq), F32),
            pltpu.VMEM((1, 2 * tq), F32),
            pltpu.VMEM((HEAD_V, 2 * tq), F32),
        ],
        compiler_params=_params(("parallel", "parallel", "arbitrary")),
        name="diff_attn",
    )(linit, q, k, vt, diff_lambda, subln)


def _mix_out_kernel(x_ref, mod_ref, yp_ref, ya_ref, w_ref, o_ref):
    width = yp_ref.shape[1]
    y = jnp.dot(yp_ref[...], w_ref[0:width, :], preferred_element_type=F32)
    y = y + jnp.dot(ya_ref[...], w_ref[width:2 * width, :], preferred_element_type=F32)
    o_ref[...] = x_ref[...] + mod_ref[5:6, :] * y


def _mix_out(x, mod, y_pool, y_attn, w_out, *, seq, tm):
    T, D = x.shape
    width = y_pool.shape[1]
    tpb = seq // tm
    return pl.pallas_call(
        _mix_out_kernel,
        grid=(T // tm,),
        in_specs=[
            pl.BlockSpec((tm, D), lambda i: (i, 0)),
            pl.BlockSpec((None, N_MOD, D), lambda i: (i // tpb, 0, 0)),
            pl.BlockSpec((tm, width), lambda i: (i, 0)),
            pl.BlockSpec((tm, width), lambda i: (i, 0)),
            _resident(w_out.shape, lambda i: (0, 0)),
        ],
        out_specs=pl.BlockSpec((tm, D), lambda i: (i, 0)),
        out_shape=jax.ShapeDtypeStruct((T, D), F32),
        compiler_params=_params(("parallel",)),
        name="mix_out",
    )(x, mod, y_pool, y_attn, w_out)


def _final_norm_kernel(x_ref, g_ref, o_ref):
    x = x_ref[...]
    o_ref[...] = x * lax.rsqrt(jnp.mean(x * x, axis=-1, keepdims=True) + RMS_EPS) * g_ref[...]


def _final_norm(x, g, *, tm):
    T, D = x.shape
    return pl.pallas_call(
        _final_norm_kernel,
        grid=(T // tm,),
        in_specs=[pl.BlockSpec((tm, D), lambda i: (i, 0)), pl.BlockSpec((1, D), lambda i: (0, 0))],
        out_specs=pl.BlockSpec((tm, D), lambda i: (i, 0)),
        out_shape=jax.ShapeDtypeStruct((T, D), F32),
        compiler_params=_params(("parallel",)),
        name="final_norm",
    )(x, g)


def _rope_tables(seq):
    pos = jnp.arange(seq, dtype=F32)
    inv_freq = ROPE_THETA ** (-jnp.arange(0, HEAD_DIM, 2, dtype=F32) / HEAD_DIM)
    ang = pos[:, None] * inv_freq[None, :]
    c, s = jnp.cos(ang), jnp.sin(ang)
    return jnp.concatenate([c, c, c, c], axis=1), jnp.concatenate([-s, s, -s, s], axis=1)


def kernel(x, c, w_mod, b_mod, norm_ffn1, ffn1_w_in, ffn1_w_out, norm_mix, w_in, pool_w, pool_b, pool_scale,
           diff_lambda, diff_subln, w_out, norm_ffn2, ffn2_w_in, ffn2_w_out, final_norm):
    B, S, D = x.shape
    L = w_mod.shape[0]
    F = ffn1_w_out.shape[1]
    tm = min(TOKEN_TILE, S)
    tq = min(Q_TILE, S)
    tk = min(KV_CHUNK, S)
    tf = FF_CHUNK if F % FF_CHUNK == 0 else F
    assert S % tm == 0 and tm % tk == 0 and tk % tq == 0
    width = pool_scale.shape[1]

    cos, sin = _rope_tables(S)
    mod_all = _modulation(c, w_mod, b_mod).reshape(L, B, N_MOD, D)
    xt = x.reshape(B * S, D)
    for l in range(L):
        mod = mod_all[l]
        linit = jnp.full((1,), 0.8 - 0.6 * math.exp(-0.3 * l), F32)
        xt = _ffn(xt, mod, norm_ffn1[l].reshape(1, D), ffn1_w_in[l].astype(BF16), ffn1_w_out[l].astype(BF16),
                  row0=0, seq=S, tm=tm, tf=tf)
        w_pqk = w_in[l, :, :3 * width].astype(BF16)
        w_vt = w_in[l, :, 3 * width:].T.astype(BF16)
        q, k, vt, y_pool = _mix_in(xt, mod, norm_mix[l].reshape(1, D), w_pqk, w_vt, cos, sin,
                                   pool_w[l].astype(BF16), pool_b[l].reshape(1, width),
                                   pool_scale[l].reshape(1, width), seq=S, tm=tm, tk=tk)
        y_attn = _attention(linit, q, k, vt, diff_lambda[l], diff_subln[l].reshape(1, HEAD_V),
                            batch=B, seq=S, tq=tq)
        xt = _mix_out(xt, mod, y_pool, y_attn, w_out[l].astype(BF16), seq=S, tm=tm)
        xt = _ffn(xt, mod, norm_ffn2[l].reshape(1, D), ffn2_w_in[l].astype(BF16), ffn2_w_out[l].astype(BF16),
                  row0=6, seq=S, tm=tm, tf=tf)
    return _final_norm(xt, final_norm.reshape(1, D), tm=tm).reshape(B, S, D)
```

```python
import functools
import math

import jax
import jax.numpy as jnp
from jax import lax
from jax.experimental import pallas as pl
from jax.experimental.pallas import tpu as pltpu

F32 = jnp.float32
BF16 = jnp.bfloat16

POOL_WINDOWS = (2, 4, 8, 16)
POOL_HALO = 16
HEAD_DIM = 64
HEAD_V = 128
ROPE_THETA = 10000.0
RMS_EPS = 1e-6
SUBLN_EPS = 1e-5
N_MOD = 9
MASK_VALUE = -1e30
Q_SCALE = HEAD_DIM ** -0.5 * math.log2(math.e)
ONES_ROWS = 16
VT_ROWS = HEAD_V + ONES_ROWS

VMEM_LIMIT_BYTES = 56 * 1024 * 1024

TOKEN_TILE = 512
FF_CHUNK = 512
Q_TILE = 512
KV_CHUNK = 1024
VT_CHUNK = 512
ATTN_SUB = 256


def _params(sem):
    return pltpu.CompilerParams(dimension_semantics=sem, vmem_limit_bytes=VMEM_LIMIT_BYTES)


def _resident(block_shape, index_map):
    return pl.BlockSpec(block_shape, index_map, pipeline_mode=pl.Buffered(1))


def _norm_modulate(x, g, shift, scale):
    y = x * lax.rsqrt(jnp.mean(x * x, axis=-1, keepdims=True) + RMS_EPS) * g
    return y * (1.0 + scale) + shift


def _mod_kernel(ct_ref, w_ref, b_ref, o_ref):
    ct = ct_ref[...]
    ct = ct * jax.nn.sigmoid(ct)
    w = w_ref[...]
    rows = [jnp.sum(w * ct[:, b:b + 1], axis=0, keepdims=True) for b in range(ct.shape[1])]
    o_ref[...] = jnp.concatenate(rows, axis=0) + b_ref[...]


def _modulation(c, w_mod, b_mod, tn=1024):
    L, D, N = w_mod.shape
    B = c.shape[0]
    return pl.pallas_call(
        _mod_kernel,
        grid=(L, N // tn),
        in_specs=[
            pl.BlockSpec((D, B), lambda l, n: (0, 0)),
            pl.BlockSpec((None, D, tn), lambda l, n: (l, 0, n)),
            pl.BlockSpec((None, 1, tn), lambda l, n: (l, 0, n)),
        ],
        out_specs=pl.BlockSpec((None, B, tn), lambda l, n: (l, 0, n)),
        out_shape=jax.ShapeDtypeStruct((L, B, N), F32),
        compiler_params=_params(("parallel", "parallel")),
        name="modulation",
    )(c.T, w_mod, b_mod.reshape(L, 1, N))


def _ffn_kernel(x_ref, mod_ref, g_ref, wg_ref, wu_ref, wo_ref, o_ref, h_ref, *, row0):
    j = pl.program_id(1)

    @pl.when(j == 0)
    def _():
        h = _norm_modulate(x_ref[...], g_ref[...], mod_ref[row0:row0 + 1, :], mod_ref[row0 + 1:row0 + 2, :])
        h_ref[...] = h.astype(BF16)
        o_ref[...] = jnp.zeros_like(o_ref)

    h = h_ref[...]
    a_g = jnp.dot(h, wg_ref[...], preferred_element_type=F32)
    a_u = jnp.dot(h, wu_ref[...], preferred_element_type=F32)
    act = (a_g * jax.nn.sigmoid(a_g) * a_u).astype(BF16)
    o_ref[...] += jnp.dot(act, wo_ref[...], preferred_element_type=F32)

    @pl.when(j == pl.num_programs(1) - 1)
    def _():
        o_ref[...] = x_ref[...] + 0.5 * mod_ref[row0 + 2:row0 + 3, :] * o_ref[...]


def _ffn(x, mod, g, w_in, w_out, *, row0, seq, tm, tf):
    T, D = x.shape
    F = w_out.shape[0]
    tpb = seq // tm
    nf = F // tf
    return pl.pallas_call(
        functools.partial(_ffn_kernel, row0=row0),
        grid=(T // tm, nf),
        in_specs=[
            pl.BlockSpec((tm, D), lambda i, j: (i, 0)),
            pl.BlockSpec((None, N_MOD, D), lambda i, j: (i // tpb, 0, 0)),
            pl.BlockSpec((1, D), lambda i, j: (0, 0)),
            pl.BlockSpec((D, tf), lambda i, j: (0, j)),
            pl.BlockSpec((D, tf), lambda i, j: (0, nf + j)),
            pl.BlockSpec((tf, D), lambda i, j: (j, 0)),
        ],
        out_specs=pl.BlockSpec((tm, D), lambda i, j: (i, 0)),
        out_shape=jax.ShapeDtypeStruct((T, D), F32),
        scratch_shapes=[pltpu.VMEM((tm, D), BF16)],
        compiler_params=_params(("parallel", "arbitrary")),
        name="ffn",
    )(x, mod, g, w_in, w_in, w_out)


def _mix_in_kernel(x_ref, mod_ref, g_ref, w_ref, wvt_ref, cos_ref, sin_ref, pw_ref, pb_ref, ps_ref,
                   q_ref, k_ref, vt_ref, yp_ref, pext_ref, *, tpb, width, n_heads):
    i = pl.program_id(0)
    tm = x_ref.shape[0]
    h = _norm_modulate(x_ref[...], g_ref[...], mod_ref[3:4, :], mod_ref[4:5, :]).astype(BF16)

    p = jnp.dot(h, w_ref[:, 0:width], preferred_element_type=F32)

    @pl.when(i % tpb == 0)
    def _():
        pext_ref[0:POOL_HALO, :] = jnp.zeros((POOL_HALO, width), F32)

    pext_ref[POOL_HALO:POOL_HALO + tm, :] = p
    pos = (i % tpb) * tm + lax.broadcasted_iota(jnp.int32, (tm, 1), 0)
    cg = width // len(POOL_WINDOWS)
    for gi, win in enumerate(POOL_WINDOWS):
        lo, hi = gi * cg, (gi + 1) * cg
        pg = p[:, lo:hi]
        acc = pg
        for back in range(1, win):
            acc = acc + pext_ref[POOL_HALO - back:POOL_HALO - back + tm, lo:hi]
        count = jnp.minimum(pos + 1, win).astype(F32)
        pooled = acc / count - pg
        mixed = jnp.dot(pooled.astype(BF16), pw_ref[gi], preferred_element_type=F32) + pb_ref[:, lo:hi]
        yp_ref[:, lo:hi] = (mixed * ps_ref[:, lo:hi]).astype(BF16)
    pext_ref[0:POOL_HALO, :] = pext_ref[tm:tm + POOL_HALO, :]

    cos = cos_ref[...]
    sin = sin_ref[...]
    lane = lax.broadcasted_iota(jnp.int32, (tm, HEAD_V), 1)
    first_half = (lane % HEAD_DIM) < (HEAD_DIM // 2)
    for sec, out_ref, mul in ((1, q_ref, Q_SCALE), (2, k_ref, None)):
        z = jnp.dot(h, w_ref[:, sec * width:(sec + 1) * width], preferred_element_type=F32)
        for hh in range(n_heads):
            zh = z[:, hh * HEAD_V:(hh + 1) * HEAD_V]
            partner = jnp.where(first_half,
                                pltpu.roll(zh, HEAD_V - HEAD_DIM // 2, axis=1),
                                pltpu.roll(zh, HEAD_DIM // 2, axis=1))
            r = zh * cos + partner * sin
            if mul is not None:
                r = r * mul
            out_ref[hh] = r.astype(BF16)

    tk = vt_ref.shape[3]
    zt = lax.dot_general(wvt_ref[...], h, (((1,), (1,)), ((), ())), preferred_element_type=F32)
    ones_rows = (lax.broadcasted_iota(jnp.int32, (ONES_ROWS, tk), 0) == 0).astype(BF16)
    for hh in range(n_heads):
        for ck in range(tm // tk):
            vt_ref[hh, ck, 0:HEAD_V, :] = zt[hh * HEAD_V:(hh + 1) * HEAD_V, ck * tk:(ck + 1) * tk].astype(BF16)
            vt_ref[hh, ck, HEAD_V:VT_ROWS, :] = ones_rows


def _mix_in(x, mod, g, w_in, w_vt, cos, sin, pool_w, pool_b, pool_scale, *, seq, tm, tk):
    T, D = x.shape
    width = w_vt.shape[0]
    n_heads = width // HEAD_V
    tpb = seq // tm
    head_major = jax.ShapeDtypeStruct((n_heads, T, HEAD_V), BF16)
    head_spec = pl.BlockSpec((n_heads, tm, HEAD_V), lambda i: (0, i, 0))
    vt_shape = jax.ShapeDtypeStruct((n_heads, T // tk, VT_ROWS, tk), BF16)
    vt_spec = pl.BlockSpec((n_heads, tm // tk, VT_ROWS, tk), lambda i: (0, i, 0, 0))
    return pl.pallas_call(
        functools.partial(_mix_in_kernel, tpb=tpb, width=width, n_heads=n_heads),
        grid=(T // tm,),
        in_specs=[
            pl.BlockSpec((tm, D), lambda i: (i, 0)),
            pl.BlockSpec((None, N_MOD, D), lambda i: (i // tpb, 0, 0)),
            pl.BlockSpec((1, D), lambda i: (0, 0)),
            _resident((D, 3 * width), lambda i: (0, 0)),
            _resident((width, D), lambda i: (0, 0)),
            pl.BlockSpec((tm, HEAD_V), lambda i: (i % tpb, 0)),
            pl.BlockSpec((tm, HEAD_V), lambda i: (i % tpb, 0)),
            _resident(pool_w.shape, lambda i: (0, 0, 0)),
            pl.BlockSpec((1, width), lambda i: (0, 0)),
            pl.BlockSpec((1, width), lambda i: (0, 0)),
        ],
        out_specs=[head_spec, head_spec, vt_spec, pl.BlockSpec((tm, width), lambda i: (i, 0))],
        out_shape=[head_major, head_major, vt_shape, jax.ShapeDtypeStruct((T, width), BF16)],
        scratch_shapes=[pltpu.VMEM((POOL_HALO + tm, width), F32)],
        compiler_params=_params(("arbitrary",)),
        name="mix_in",
    )(x, mod, g, w_in, w_vt, cos, sin, pool_w, pool_b, pool_scale)


def _attn_kernel(linit_ref, q_ref, k_ref, vt_ref, dl_ref, g_ref, o_ref,
                 qs_ref, sa_ref, sb_ref, mca_ref, mcb_ref, m_ref, acc_ref, *, tk):
    i = pl.program_id(2)
    tq = q_ref.shape[0]
    vc = vt_ref.shape[2]

    q = q_ref[...]
    lane = lax.broadcasted_iota(jnp.int32, q.shape, 1)
    qs_ref[0:tq, :] = jnp.where(lane < HEAD_DIM, q, jnp.zeros_like(q))
    qs_ref[tq:2 * tq, :] = jnp.where(lane < HEAD_DIM, jnp.zeros_like(q), q)
    m_ref[...] = jnp.full(m_ref.shape, MASK_VALUE, F32)
    acc_ref[...] = jnp.zeros(acc_ref.shape, F32)

    last = (i * tq) // tk

    buf_a = (sa_ref, mca_ref)
    buf_b = (sb_ref, mcb_ref)

    sub = min(ATTN_SUB, vc)
    n_sub = tk // sub

    def score_sub(j, u, masked):
        start = pl.multiple_of(j * tk + u * sub, sub)
        s = lax.dot_general(k_ref[pl.ds(start, sub), :], qs_ref[...], (((1,), (1,)), ((), ())),
                            preferred_element_type=F32)
        if masked:
            key_pos = start + lax.broadcasted_iota(jnp.int32, s.shape, 0)
            col = lax.broadcasted_iota(jnp.int32, s.shape, 1)
            q_pos = i * tq + jnp.where(col >= tq, col - tq, col)
            s = jnp.where(key_pos <= q_pos, s, MASK_VALUE)
        return s

    def step(j, cur, nxt, next_masked):
        if cur is not None:
            s_cur, mc_cur = cur
            m_prev = m_ref[...]
            m_new = jnp.maximum(m_prev, mc_cur[...])
            alpha = jnp.exp2(m_prev - m_new)
        pv = None
        mx = None
        for u in range(n_sub):
            if cur is not None:
                p = jnp.exp2(s_cur[u * sub:(u + 1) * sub, :] - m_new).astype(BF16)
                lo = (u * sub) % vc
                vt = vt_ref[j * (tk // vc) + (u * sub) // vc, :, lo:lo + sub]
                d = jnp.dot(vt, p, preferred_element_type=F32)
                pv = d if pv is None else pv + d
            if nxt is not None:
                s = score_sub(j + 1 if cur is not None else j, u, next_masked)
                nxt[0][u * sub:(u + 1) * sub, :] = s
                smax = jnp.max(s, axis=0, keepdims=True)
                mx = smax if mx is None else jnp.maximum(mx, smax)
        if nxt is not None:
            nxt[1][...] = mx
        if cur is not None:
            acc_ref[...] = alpha * acc_ref[...] + pv
            m_ref[...] = m_new

    @pl.when(last == 0)
    def _():
        step(0, None, buf_a, True)

    @pl.when(last > 0)
    def _():
        step(0, None, buf_a, False)

    trips = jnp.maximum(last - 1, 0) // 2

    def body(t, carry):
        step(2 * t, buf_a, buf_b, False)
        step(2 * t + 1, buf_b, buf_a, False)
        return carry

    lax.fori_loop(0, trips, body, 0)
    done = 2 * trips

    @pl.when(last == done)
    def _():
        step(last, buf_a, None, None)

    @pl.when(last == done + 1)
    def _():
        step(done, buf_a, buf_b, True)
        step(last, buf_b, None, None)

    @pl.when(last == done + 2)
    def _():
        step(done, buf_a, buf_b, False)
        step(done + 1, buf_b, buf_a, True)
        step(last, buf_a, None, None)

    dl = dl_ref[...]
    lambda_init = linit_ref[0]
    lam = (jnp.exp(jnp.sum(dl[0:1] * dl[1:2], axis=1, keepdims=True))
           - jnp.exp(jnp.sum(dl[2:3] * dl[3:4], axis=1, keepdims=True)) + lambda_init)
    num = acc_ref[0:HEAD_V, :]
    den = acc_ref[HEAD_V:HEAD_V + 1, :]
    o = num[:, 0:tq] / den[:, 0:tq] - lam * (num[:, tq:2 * tq] / den[:, tq:2 * tq])
    o = o * lax.rsqrt(jnp.mean(o * o, axis=0, keepdims=True) + SUBLN_EPS)
    o_ref[...] = (o.T * g_ref[...] * (1.0 - lambda_init)).astype(BF16)


def _attention(linit, q, k, vt, diff_lambda, subln, *, batch, seq, tq, tk):
    n_heads, T, _ = q.shape
    vc = vt.shape[3]
    nq = seq // tq
    nk = seq // vc
    return pl.pallas_call(
        functools.partial(_attn_kernel, tk=tk),
        grid=(batch, n_heads, nq),
        in_specs=[
            pl.BlockSpec(memory_space=pltpu.SMEM),
            pl.BlockSpec((None, tq, HEAD_V), lambda b, h, i: (h, b * nq + i, 0)),
            pl.BlockSpec((None, seq, HEAD_V), lambda b, h, i: (h, b, 0)),
            pl.BlockSpec((None, nk, VT_ROWS, vc), lambda b, h, i: (h, b, 0, 0)),
            pl.BlockSpec(diff_lambda.shape, lambda b, h, i: (0, 0)),
            pl.BlockSpec((1, HEAD_V), lambda b, h, i: (0, 0)),
        ],
        out_specs=pl.BlockSpec((tq, HEAD_V), lambda b, h, i: (b * nq + i, h)),
        out_shape=jax.ShapeDtypeStruct((T, n_heads * HEAD_V), BF16),
        scratch_shapes=[
            pltpu.VMEM((2 * tq, HEAD_V), BF16),
            pltpu.VMEM((tk, 2 * tq), F32),
            pltpu.VMEM((tk, 2 * tq), F32),
            pltpu.VMEM((1, 2 * tq), F32),
            pltpu.VMEM((1, 2 * tq), F32),
            pltpu.VMEM((1, 2 * tq), F32),
            pltpu.VMEM((VT_ROWS, 2 * tq), F32),
        ],
        compiler_params=_params(("parallel", "parallel", "arbitrary")),
        name="diff_attn",
    )(linit, q, k, vt, diff_lambda, subln)


def _mix_out_kernel(x_ref, mod_ref, yp_ref, ya_ref, w_ref, o_ref):
    width = yp_ref.shape[1]
    y = jnp.dot(yp_ref[...], w_ref[0:width, :], preferred_element_type=F32)
    y = y + jnp.dot(ya_ref[...], w_ref[width:2 * width, :], preferred_element_type=F32)
    o_ref[...] = x_ref[...] + mod_ref[5:6, :] * y


def _mix_out(x, mod, y_pool, y_attn, w_out, *, seq, tm):
    T, D = x.shape
    width = y_pool.shape[1]
    tpb = seq // tm
    return pl.pallas_call(
        _mix_out_kernel,
        grid=(T // tm,),
        in_specs=[
            pl.BlockSpec((tm, D), lambda i: (i, 0)),
            pl.BlockSpec((None, N_MOD, D), lambda i: (i // tpb, 0, 0)),
            pl.BlockSpec((tm, width), lambda i: (i, 0)),
            pl.BlockSpec((tm, width), lambda i: (i, 0)),
            _resident(w_out.shape, lambda i: (0, 0)),
        ],
        out_specs=pl.BlockSpec((tm, D), lambda i: (i, 0)),
        out_shape=jax.ShapeDtypeStruct((T, D), F32),
        compiler_params=_params(("parallel",)),
        name="mix_out",
    )(x, mod, y_pool, y_attn, w_out)


def _final_norm_kernel(x_ref, g_ref, o_ref):
    x = x_ref[...]
    o_ref[...] = x * lax.rsqrt(jnp.mean(x * x, axis=-1, keepdims=True) + RMS_EPS) * g_ref[...]


def _final_norm(x, g, *, tm):
    T, D = x.shape
    return pl.pallas_call(
        _final_norm_kernel,
        grid=(T // tm,),
        in_specs=[pl.BlockSpec((tm, D), lambda i: (i, 0)), pl.BlockSpec((1, D), lambda i: (0, 0))],
        out_specs=pl.BlockSpec((tm, D), lambda i: (i, 0)),
        out_shape=jax.ShapeDtypeStruct((T, D), F32),
        compiler_params=_params(("parallel",)),
        name="final_norm",
    )(x, g)


def _rope_tables(seq):
    pos = jnp.arange(seq, dtype=F32)
    inv_freq = ROPE_THETA ** (-jnp.arange(0, HEAD_DIM, 2, dtype=F32) / HEAD_DIM)
    ang = pos[:, None] * inv_freq[None, :]
    c, s = jnp.cos(ang), jnp.sin(ang)
    return jnp.concatenate([c, c, c, c], axis=1), jnp.concatenate([-s, s, -s, s], axis=1)


def kernel(x, c, w_mod, b_mod, norm_ffn1, ffn1_w_in, ffn1_w_out, norm_mix, w_in, pool_w, pool_b, pool_scale,
           diff_lambda, diff_subln, w_out, norm_ffn2, ffn2_w_in, ffn2_w_out, final_norm):
    B, S, D = x.shape
    L = w_mod.shape[0]
    F = ffn1_w_out.shape[1]
    tm = min(TOKEN_TILE, S)
    tq = min(Q_TILE, S)
    tk = min(KV_CHUNK, S)
    vc = min(VT_CHUNK, tk, tm)
    tf = FF_CHUNK if F % FF_CHUNK == 0 else F
    assert S % tm == 0 and S % tk == 0 and tm % vc == 0 and tk % vc == 0 and tk % tq == 0
    width = pool_scale.shape[1]

    cos, sin = _rope_tables(S)
    mod_all = _modulation(c, w_mod, b_mod).reshape(L, B, N_MOD, D)
    xt = x.reshape(B * S, D)
    for l in range(L):
        mod = mod_all[l]
        linit = jnp.full((1,), 0.8 - 0.6 * math.exp(-0.3 * l), F32)
        xt = _ffn(xt, mod, norm_ffn1[l].reshape(1, D), ffn1_w_in[l].astype(BF16), ffn1_w_out[l].astype(BF16),
                  row0=0, seq=S, tm=tm, tf=tf)
        w_pqk = w_in[l, :, :3 * width].astype(BF16)
        w_vt = w_in[l, :, 3 * width:].T.astype(BF16)
        q, k, vt, y_pool = _mix_in(xt, mod, norm_mix[l].reshape(1, D), w_pqk, w_vt, cos, sin,
                                   pool_w[l].astype(BF16), pool_b[l].reshape(1, width),
                                   pool_scale[l].reshape(1, width), seq=S, tm=tm, tk=vc)
        y_attn = _attention(linit, q, k, vt, diff_lambda[l], diff_subln[l].reshape(1, HEAD_V),
                            batch=B, seq=S, tq=tq, tk=tk)
        xt = _mix_out(xt, mod, y_pool, y_attn, w_out[l].astype(BF16), seq=S, tm=tm)
        xt = _ffn(xt, mod, norm_ffn2[l].reshape(1, D), ffn2_w_in[l].astype(BF16), ffn2_w_out[l].astype(BF16),
                  row0=6, seq=S, tm=tm, tf=tf)
    return _final_norm(xt, final_norm.reshape(1, D), tm=tm).reshape(B, S, D)
```

```python
import functools
import math

import jax
import jax.numpy as jnp
from jax import lax
from jax.experimental import pallas as pl
from jax.experimental.pallas import tpu as pltpu

F32 = jnp.float32
BF16 = jnp.bfloat16

POOL_WINDOWS = (2, 4, 8, 16)
POOL_HALO = 16
HEAD_DIM = 64
HEAD_V = 128
ROPE_THETA = 10000.0
RMS_EPS = 1e-6
SUBLN_EPS = 1e-5
N_MOD = 9
MASK_VALUE = -1e30
Q_SCALE = HEAD_DIM ** -0.5 * math.log2(math.e)
ONES_ROWS = 16
VT_ROWS = HEAD_V + ONES_ROWS

VMEM_LIMIT_BYTES = 56 * 1024 * 1024

TOKEN_TILE = 512
FF_CHUNK = 512
FFN_NORM_PARTS = 8
Q_TILE = 512
KV_CHUNK = 1024
VT_CHUNK = 512
ATTN_SUB = 256


def _params(sem):
    return pltpu.CompilerParams(dimension_semantics=sem, vmem_limit_bytes=VMEM_LIMIT_BYTES)


def _resident(block_shape, index_map):
    return pl.BlockSpec(block_shape, index_map, pipeline_mode=pl.Buffered(1))


def _norm_modulate(x, g, shift, scale):
    y = x * lax.rsqrt(jnp.mean(x * x, axis=-1, keepdims=True) + RMS_EPS) * g
    return y * (1.0 + scale) + shift


def _mod_kernel(ct_ref, w_ref, b_ref, o_ref):
    ct = ct_ref[...]
    ct = ct * jax.nn.sigmoid(ct)
    w = w_ref[...]
    rows = [jnp.sum(w * ct[:, b:b + 1], axis=0, keepdims=True) for b in range(ct.shape[1])]
    o_ref[...] = jnp.concatenate(rows, axis=0) + b_ref[...]


def _modulation(c, w_mod, b_mod, tn=1024):
    L, D, N = w_mod.shape
    B = c.shape[0]
    return pl.pallas_call(
        _mod_kernel,
        grid=(L, N // tn),
        in_specs=[
            pl.BlockSpec((D, B), lambda l, n: (0, 0)),
            pl.BlockSpec((None, D, tn), lambda l, n: (l, 0, n)),
            pl.BlockSpec((None, 1, tn), lambda l, n: (l, 0, n)),
        ],
        out_specs=pl.BlockSpec((None, B, tn), lambda l, n: (l, 0, n)),
        out_shape=jax.ShapeDtypeStruct((L, B, N), F32),
        compiler_params=_params(("parallel", "parallel")),
        name="modulation",
    )(c.T, w_mod, b_mod.reshape(L, 1, N))


def _ffn_kernel(x_ref, xn_ref, mod_ref, modn_ref, g_ref, win_hbm, wout_hbm, o_ref,
                h_ref, hn_ref, wg_buf, wu_buf, wo_buf, sem, *, row0, nf, tf):
    i = pl.program_id(0)
    n_tiles = pl.num_programs(0)

    def chunk_copies(j, slot):
        col = j * tf if isinstance(j, int) else pl.multiple_of(j * tf, tf)
        return (
            pltpu.make_async_copy(win_hbm.at[:, pl.ds(col, tf)], wg_buf.at[slot], sem.at[slot, 0]),
            pltpu.make_async_copy(win_hbm.at[:, pl.ds(nf * tf + col, tf)], wu_buf.at[slot], sem.at[slot, 1]),
            pltpu.make_async_copy(wout_hbm.at[pl.ds(col, tf), :], wo_buf.at[slot], sem.at[slot, 2]),
        )

    def normed(xs, ms):
        return _norm_modulate(xs, g_ref[...], ms[row0:row0 + 1, :], ms[row0 + 1:row0 + 2, :]).astype(BF16)

    @pl.when(i == 0)
    def _():
        for cp in chunk_copies(0, 0):
            cp.start()
        h_ref[...] = normed(x_ref[...], mod_ref)

    @pl.when(i > 0)
    def _():
        h_ref[...] = hn_ref[...]

    o_ref[...] = jnp.zeros_like(o_ref)
    parts = FFN_NORM_PARTS if nf >= FFN_NORM_PARTS else 1
    rows = x_ref.shape[0] // parts

    def body(j, carry):
        slot = (i * nf + j) % 2

        @pl.when(jnp.logical_or(j + 1 < nf, i + 1 < n_tiles))
        def _():
            for cp in chunk_copies(jnp.where(j + 1 < nf, j + 1, 0), 1 - slot):
                cp.start()

        for cp in chunk_copies(j, slot):
            cp.wait()
        r0 = pl.multiple_of(jnp.minimum(j, parts - 1) * rows, rows)
        hn_ref[pl.ds(r0, rows), :] = normed(xn_ref[pl.ds(r0, rows), :], modn_ref)
        hh = h_ref[...]
        a_g = jnp.dot(hh, wg_buf[slot], preferred_element_type=F32)
        a_u = jnp.dot(hh, wu_buf[slot], preferred_element_type=F32)
        act = (a_g * jax.nn.sigmoid(a_g) * a_u).astype(BF16)
        o_ref[...] += jnp.dot(act, wo_buf[slot], preferred_element_type=F32)
        return carry

    lax.fori_loop(0, nf, body, 0)
    o_ref[...] = x_ref[...] + 0.5 * mod_ref[row0 + 2:row0 + 3, :] * o_ref[...]


def _ffn(x, mod, g, w_in, w_out, *, row0, seq, tm, tf):
    T, D = x.shape
    F = w_out.shape[0]
    tpb = seq // tm
    nf = F // tf
    n_tiles = T // tm
    return pl.pallas_call(
        functools.partial(_ffn_kernel, row0=row0, nf=nf, tf=tf),
        grid=(n_tiles,),
        in_specs=[
            pl.BlockSpec((tm, D), lambda i: (i, 0)),
            pl.BlockSpec((tm, D), lambda i: (jnp.minimum(i + 1, n_tiles - 1), 0)),
            pl.BlockSpec((None, N_MOD, D), lambda i: (i // tpb, 0, 0)),
            pl.BlockSpec((None, N_MOD, D), lambda i: (jnp.minimum(i + 1, n_tiles - 1) // tpb, 0, 0)),
            pl.BlockSpec((1, D), lambda i: (0, 0)),
            pl.BlockSpec(memory_space=pl.ANY),
            pl.BlockSpec(memory_space=pl.ANY),
        ],
        out_specs=pl.BlockSpec((tm, D), lambda i: (i, 0)),
        out_shape=jax.ShapeDtypeStruct((T, D), F32),
        scratch_shapes=[
            pltpu.VMEM((tm, D), BF16),
            pltpu.VMEM((tm, D), BF16),
            pltpu.VMEM((2, D, tf), BF16),
            pltpu.VMEM((2, D, tf), BF16),
            pltpu.VMEM((2, tf, D), BF16),
            pltpu.SemaphoreType.DMA((2, 3)),
        ],
        compiler_params=_params(("arbitrary",)),
        name="ffn",
    )(x, x, mod, mod, g, w_in, w_out)


def _mix_in_kernel(x_ref, mod_ref, g_ref, w_ref, wvt_ref, cos_ref, sin_ref, pw_ref, pb_ref, ps_ref,
                   q_ref, k_ref, vt_ref, yp_ref, pext_ref, *, tpb, width, n_heads):
    i = pl.program_id(0)
    tm = x_ref.shape[0]
    h = _norm_modulate(x_ref[...], g_ref[...], mod_ref[3:4, :], mod_ref[4:5, :]).astype(BF16)

    p = jnp.dot(h, w_ref[:, 0:width], preferred_element_type=F32)

    @pl.when(i % tpb == 0)
    def _():
        pext_ref[0:POOL_HALO, :] = jnp.zeros((POOL_HALO, width), F32)

    pext_ref[POOL_HALO:POOL_HALO + tm, :] = p
    pos = (i % tpb) * tm + lax.broadcasted_iota(jnp.int32, (tm, 1), 0)
    cg = width // len(POOL_WINDOWS)
    for gi, win in enumerate(POOL_WINDOWS):
        lo, hi = gi * cg, (gi + 1) * cg
        pg = p[:, lo:hi]
        acc = pg
        for back in range(1, win):
            acc = acc + pext_ref[POOL_HALO - back:POOL_HALO - back + tm, lo:hi]
        count = jnp.minimum(pos + 1, win).astype(F32)
        pooled = acc / count - pg
        mixed = jnp.dot(pooled.astype(BF16), pw_ref[gi], preferred_element_type=F32) + pb_ref[:, lo:hi]
        yp_ref[:, lo:hi] = (mixed * ps_ref[:, lo:hi]).astype(BF16)
    pext_ref[0:POOL_HALO, :] = pext_ref[tm:tm + POOL_HALO, :]

    cos = cos_ref[...]
    sin = sin_ref[...]
    lane = lax.broadcasted_iota(jnp.int32, (tm, HEAD_V), 1)
    first_half = (lane % HEAD_DIM) < (HEAD_DIM // 2)
    for sec, out_ref, mul in ((1, q_ref, Q_SCALE), (2, k_ref, None)):
        z = jnp.dot(h, w_ref[:, sec * width:(sec + 1) * width], preferred_element_type=F32)
        for hh in range(n_heads):
            zh = z[:, hh * HEAD_V:(hh + 1) * HEAD_V]
            partner = jnp.where(first_half,
                                pltpu.roll(zh, HEAD_V - HEAD_DIM // 2, axis=1),
                                pltpu.roll(zh, HEAD_DIM // 2, axis=1))
            r = zh * cos + partner * sin
            if mul is not None:
                r = r * mul
            out_ref[hh] = r.astype(BF16)

    tk = vt_ref.shape[3]
    zt = lax.dot_general(wvt_ref[...], h, (((1,), (1,)), ((), ())), preferred_element_type=F32)
    ones_rows = (lax.broadcasted_iota(jnp.int32, (ONES_ROWS, tk), 0) == 0).astype(BF16)
    for hh in range(n_heads):
        for ck in range(tm // tk):
            vt_ref[hh, ck, 0:HEAD_V, :] = zt[hh * HEAD_V:(hh + 1) * HEAD_V, ck * tk:(ck + 1) * tk].astype(BF16)
            vt_ref[hh, ck, HEAD_V:VT_ROWS, :] = ones_rows


def _mix_in(x, mod, g, w_in, w_vt, cos, sin, pool_w, pool_b, pool_scale, *, seq, tm, tk):
    T, D = x.shape
    width = w_vt.shape[0]
    n_heads = width // HEAD_V
    tpb = seq // tm
    head_major = jax.ShapeDtypeStruct((n_heads, T, HEAD_V), BF16)
    head_spec = pl.BlockSpec((n_heads, tm, HEAD_V), lambda i: (0, i, 0))
    vt_shape = jax.ShapeDtypeStruct((n_heads, T // tk, VT_ROWS, tk), BF16)
    vt_spec = pl.BlockSpec((n_heads, tm // tk, VT_ROWS, tk), lambda i: (0, i, 0, 0))
    return pl.pallas_call(
        functools.partial(_mix_in_kernel, tpb=tpb, width=width, n_heads=n_heads),
        grid=(T // tm,),
        in_specs=[
            pl.BlockSpec((tm, D), lambda i: (i, 0)),
            pl.BlockSpec((None, N_MOD, D), lambda i: (i // tpb, 0, 0)),
            pl.BlockSpec((1, D), lambda i: (0, 0)),
            _resident((D, 3 * width), lambda i: (0, 0)),
            _resident((width, D), lambda i: (0, 0)),
            pl.BlockSpec((tm, HEAD_V), lambda i: (i % tpb, 0)),
            pl.BlockSpec((tm, HEAD_V), lambda i: (i % tpb, 0)),
            _resident(pool_w.shape, lambda i: (0, 0, 0)),
            pl.BlockSpec((1, width), lambda i: (0, 0)),
            pl.BlockSpec((1, width), lambda i: (0, 0)),
        ],
        out_specs=[head_spec, head_spec, vt_spec, pl.BlockSpec((tm, width), lambda i: (i, 0))],
        out_shape=[head_major, head_major, vt_shape, jax.ShapeDtypeStruct((T, width), BF16)],
        scratch_shapes=[pltpu.VMEM((POOL_HALO + tm, width), F32)],
        compiler_params=_params(("arbitrary",)),
        name="mix_in",
    )(x, mod, g, w_in, w_vt, cos, sin, pool_w, pool_b, pool_scale)


def _attn_kernel(linit_ref, q_ref, k_ref, vt_ref, dl_ref, g_ref, o_ref,
                 qs_ref, sa_ref, sb_ref, mca_ref, mcb_ref, m_ref, acc_ref, *, tk):
    i = pl.program_id(2)
    tq = q_ref.shape[0]
    vc = vt_ref.shape[2]

    q = q_ref[...]
    lane = lax.broadcasted_iota(jnp.int32, q.shape, 1)
    qs_ref[0:tq, :] = jnp.where(lane < HEAD_DIM, q, jnp.zeros_like(q))
    qs_ref[tq:2 * tq, :] = jnp.where(lane < HEAD_DIM, jnp.zeros_like(q), q)
    m_ref[...] = jnp.full(m_ref.shape, MASK_VALUE, F32)
    acc_ref[...] = jnp.zeros(acc_ref.shape, F32)

    last = (i * tq) // tk

    buf_a = (sa_ref, mca_ref)
    buf_b = (sb_ref, mcb_ref)

    sub = min(ATTN_SUB, vc)
    n_sub = tk // sub

    def score_sub(j, u, masked):
        start = pl.multiple_of(j * tk + u * sub, sub)
        s = lax.dot_general(k_ref[pl.ds(start, sub), :], qs_ref[...], (((1,), (1,)), ((), ())),
                            preferred_element_type=F32)
        if masked:
            key_pos = start + lax.broadcasted_iota(jnp.int32, s.shape, 0)
            col = lax.broadcasted_iota(jnp.int32, s.shape, 1)
            q_pos = i * tq + jnp.where(col >= tq, col - tq, col)
            s = jnp.where(key_pos <= q_pos, s, MASK_VALUE)
        return s

    def step(j, cur, nxt, next_masked):
        if cur is not None:
            s_cur, mc_cur = cur
            m_prev = m_ref[...]
            m_new = jnp.maximum(m_prev, mc_cur[...])
            alpha = jnp.exp2(m_prev - m_new)
        pv = None
        mx = None
        for u in range(n_sub):
            if nxt is not None:
                s = score_sub(j + 1 if cur is not None else j, u, next_masked)
                nxt[0][u * sub:(u + 1) * sub, :] = s
                smax = jnp.max(s, axis=0, keepdims=True)
                mx = smax if mx is None else jnp.maximum(mx, smax)
            if cur is not None:
                p = jnp.exp2(s_cur[u * sub:(u + 1) * sub, :] - m_new).astype(BF16)
                lo = (u * sub) % vc
                vt = vt_ref[j * (tk // vc) + (u * sub) // vc, :, lo:lo + sub]
                d = jnp.dot(vt, p, preferred_element_type=F32)
                pv = d if pv is None else pv + d
        if nxt is not None:
            nxt[1][...] = mx
        if cur is not None:
            acc_ref[...] = alpha * acc_ref[...] + pv
            m_ref[...] = m_new

    @pl.when(last == 0)
    def _():
        step(0, None, buf_a, True)

    @pl.when(last > 0)
    def _():
        step(0, None, buf_a, False)

    trips = jnp.maximum(last - 1, 0) // 2

    def body(t, carry):
        step(2 * t, buf_a, buf_b, False)
        step(2 * t + 1, buf_b, buf_a, False)
        return carry

    lax.fori_loop(0, trips, body, 0)
    done = 2 * trips

    @pl.when(last == done)
    def _():
        step(last, buf_a, None, None)

    @pl.when(last == done + 1)
    def _():
        step(done, buf_a, buf_b, True)
        step(last, buf_b, None, None)

    @pl.when(last == done + 2)
    def _():
        step(done, buf_a, buf_b, False)
        step(done + 1, buf_b, buf_a, True)
        step(last, buf_a, None, None)

    dl = dl_ref[...]
    lambda_init = linit_ref[0]
    lam = (jnp.exp(jnp.sum(dl[0:1] * dl[1:2], axis=1, keepdims=True))
           - jnp.exp(jnp.sum(dl[2:3] * dl[3:4], axis=1, keepdims=True)) + lambda_init)
    num = acc_ref[0:HEAD_V, :]
    den = acc_ref[HEAD_V:HEAD_V + 1, :]
    o = num[:, 0:tq] / den[:, 0:tq] - lam * (num[:, tq:2 * tq] / den[:, tq:2 * tq])
    o = o * lax.rsqrt(jnp.mean(o * o, axis=0, keepdims=True) + SUBLN_EPS)
    o_ref[...] = (o.T * g_ref[...] * (1.0 - lambda_init)).astype(BF16)


def _attention(linit, q, k, vt, diff_lambda, subln, *, batch, seq, tq, tk):
    n_heads, T, _ = q.shape
    vc = vt.shape[3]
    nq = seq // tq
    nk = seq // vc
    return pl.pallas_call(
        functools.partial(_attn_kernel, tk=tk),
        grid=(batch, n_heads, nq),
        in_specs=[
            pl.BlockSpec(memory_space=pltpu.SMEM),
            pl.BlockSpec((None, tq, HEAD_V), lambda b, h, i: (h, b * nq + i, 0)),
            pl.BlockSpec((None, seq, HEAD_V), lambda b, h, i: (h, b, 0)),
            pl.BlockSpec((None, nk, VT_ROWS, vc), lambda b, h, i: (h, b, 0, 0)),
            pl.BlockSpec(diff_lambda.shape, lambda b, h, i: (0, 0)),
            pl.BlockSpec((1, HEAD_V), lambda b, h, i: (0, 0)),
        ],
        out_specs=pl.BlockSpec((tq, HEAD_V), lambda b, h, i: (b * nq + i, h)),
        out_shape=jax.ShapeDtypeStruct((T, n_heads * HEAD_V), BF16),
        scratch_shapes=[
            pltpu.VMEM((2 * tq, HEAD_V), BF16),
            pltpu.VMEM((tk, 2 * tq), F32),
            pltpu.VMEM((tk, 2 * tq), F32),
            pltpu.VMEM((1, 2 * tq), F32),
            pltpu.VMEM((1, 2 * tq), F32),
            pltpu.VMEM((1, 2 * tq), F32),
            pltpu.VMEM((VT_ROWS, 2 * tq), F32),
        ],
        compiler_params=_params(("parallel", "parallel", "arbitrary")),
        name="diff_attn",
    )(linit, q, k, vt, diff_lambda, subln)


def _mix_out_kernel(x_ref, mod_ref, yp_ref, ya_ref, w_ref, o_ref):
    width = yp_ref.shape[1]
    y = jnp.dot(yp_ref[...], w_ref[0:width, :], preferred_element_type=F32)
    y = y + jnp.dot(ya_ref[...], w_ref[width:2 * width, :], preferred_element_type=F32)
    o_ref[...] = x_ref[...] + mod_ref[5:6, :] * y


def _mix_out(x, mod, y_pool, y_attn, w_out, *, seq, tm):
    T, D = x.shape
    width = y_pool.shape[1]
    tpb = seq // tm
    return pl.pallas_call(
        _mix_out_kernel,
        grid=(T // tm,),
        in_specs=[
            pl.BlockSpec((tm, D), lambda i: (i, 0)),
            pl.BlockSpec((None, N_MOD, D), lambda i: (i // tpb, 0, 0)),
            pl.BlockSpec((tm, width), lambda i: (i, 0)),
            pl.BlockSpec((tm, width), lambda i: (i, 0)),
            _resident(w_out.shape, lambda i: (0, 0)),
        ],
        out_specs=pl.BlockSpec((tm, D), lambda i: (i, 0)),
        out_shape=jax.ShapeDtypeStruct((T, D), F32),
        compiler_params=_params(("parallel",)),
        name="mix_out",
    )(x, mod, y_pool, y_attn, w_out)


def _final_norm_kernel(x_ref, g_ref, o_ref):
    x = x_ref[...]
    o_ref[...] = x * lax.rsqrt(jnp.mean(x * x, axis=-1, keepdims=True) + RMS_EPS) * g_ref[...]


def _final_norm(x, g, *, tm):
    T, D = x.shape
    return pl.pallas_call(
        _final_norm_kernel,
        grid=(T // tm,),
        in_specs=[pl.BlockSpec((tm, D), lambda i: (i, 0)), pl.BlockSpec((1, D), lambda i: (0, 0))],
        out_specs=pl.BlockSpec((tm, D), lambda i: (i, 0)),
        out_shape=jax.ShapeDtypeStruct((T, D), F32),
        compiler_params=_params(("parallel",)),
        name="final_norm",
    )(x, g)


def _rope_tables(seq):
    pos = jnp.arange(seq, dtype=F32)
    inv_freq = ROPE_THETA ** (-jnp.arange(0, HEAD_DIM, 2, dtype=F32) / HEAD_DIM)
    ang = pos[:, None] * inv_freq[None, :]
    c, s = jnp.cos(ang), jnp.sin(ang)
    return jnp.concatenate([c, c, c, c], axis=1), jnp.concatenate([-s, s, -s, s], axis=1)


def kernel(x, c, w_mod, b_mod, norm_ffn1, ffn1_w_in, ffn1_w_out, norm_mix, w_in, pool_w, pool_b, pool_scale,
           diff_lambda, diff_subln, w_out, norm_ffn2, ffn2_w_in, ffn2_w_out, final_norm):
    B, S, D = x.shape
    L = w_mod.shape[0]
    F = ffn1_w_out.shape[1]
    tm = min(TOKEN_TILE, S)
    tq = min(Q_TILE, S)
    tk = min(KV_CHUNK, S)
    vc = min(VT_CHUNK, tk, tm)
    tf = FF_CHUNK if F % FF_CHUNK == 0 else F
    assert S % tm == 0 and S % tk == 0 and tm % vc == 0 and tk % vc == 0 and tk % tq == 0
    width = pool_scale.shape[1]

    cos, sin = _rope_tables(S)
    mod_all = _modulation(c, w_mod, b_mod).reshape(L, B, N_MOD, D)
    xt = x.reshape(B * S, D)
    for l in range(L):
        mod = mod_all[l]
        linit = jnp.full((1,), 0.8 - 0.6 * math.exp(-0.3 * l), F32)
        xt = _ffn(xt, mod, norm_ffn1[l].reshape(1, D), ffn1_w_in[l].astype(BF16), ffn1_w_out[l].astype(BF16),
                  row0=0, seq=S, tm=tm, tf=tf)
        w_pqk = w_in[l, :, :3 * width].astype(BF16)
        w_vt = w_in[l, :, 3 * width:].T.astype(BF16)
        q, k, vt, y_pool = _mix_in(xt, mod, norm_mix[l].reshape(1, D), w_pqk, w_vt, cos, sin,
                                   pool_w[l].astype(BF16), pool_b[l].reshape(1, width),
                                   pool_scale[l].reshape(1, width), seq=S, tm=tm, tk=vc)
        y_attn = _attention(linit, q, k, vt, diff_lambda[l], diff_subln[l].reshape(1, HEAD_V),
                            batch=B, seq=S, tq=tq, tk=tk)
        xt = _mix_out(xt, mod, y_pool, y_attn, w_out[l].astype(BF16), seq=S, tm=tm)
        xt = _ffn(xt, mod, norm_ffn2[l].reshape(1, D), ffn2_w_in[l].astype(BF16), ffn2_w_out[l].astype(BF16),
                  row0=6, seq=S, tm=tm, tf=tf)
    return _final_norm(xt, final_norm.reshape(1, D), tm=tm).reshape(B, S, D)
```

```python
import functools
import math

import jax
import jax.numpy as jnp
from jax import lax
from jax.experimental import pallas as pl
from jax.experimental.pallas import tpu as pltpu

F32 = jnp.float32
BF16 = jnp.bfloat16

POOL_WINDOWS = (2, 4, 8, 16)
POOL_HALO = 16
HEAD_DIM = 64
HEAD_V = 128
ROPE_THETA = 10000.0
RMS_EPS = 1e-6
SUBLN_EPS = 1e-5
N_MOD = 9
MASK_VALUE = -1e30
Q_SCALE = HEAD_DIM ** -0.5 * math.log2(math.e)
ONES_ROWS = 16
VT_ROWS = HEAD_V + ONES_ROWS

VMEM_LIMIT_BYTES = 56 * 1024 * 1024

TOKEN_TILE = 512
FF_CHUNK = 512
FFN_NORM_PARTS = 8
Q_TILE = 512
KV_CHUNK = 1024
VT_CHUNK = 512
ATTN_SUB = 256


def _params(sem):
    return pltpu.CompilerParams(dimension_semantics=sem, vmem_limit_bytes=VMEM_LIMIT_BYTES)


def _resident(block_shape, index_map):
    return pl.BlockSpec(block_shape, index_map, pipeline_mode=pl.Buffered(1))


def _norm_modulate(x, g, shift, scale):
    y = x * lax.rsqrt(jnp.mean(x * x, axis=-1, keepdims=True) + RMS_EPS) * g
    return y * (1.0 + scale) + shift


def _mod_kernel(ct_ref, w_ref, b_ref, o_ref):
    ct = ct_ref[...]
    ct = ct * jax.nn.sigmoid(ct)
    w = w_ref[...]
    rows = [jnp.sum(w * ct[:, b:b + 1], axis=0, keepdims=True) for b in range(ct.shape[1])]
    o_ref[...] = jnp.concatenate(rows, axis=0) + b_ref[...]


def _modulation(c, w_mod, b_mod, tn=1024):
    L, D, N = w_mod.shape
    B = c.shape[0]
    return pl.pallas_call(
        _mod_kernel,
        grid=(L, N // tn),
        in_specs=[
            pl.BlockSpec((D, B), lambda l, n: (0, 0)),
            pl.BlockSpec((None, D, tn), lambda l, n: (l, 0, n)),
            pl.BlockSpec((None, 1, tn), lambda l, n: (l, 0, n)),
        ],
        out_specs=pl.BlockSpec((None, B, tn), lambda l, n: (l, 0, n)),
        out_shape=jax.ShapeDtypeStruct((L, B, N), F32),
        compiler_params=_params(("parallel", "parallel")),
        name="modulation",
    )(c.T, w_mod, b_mod.reshape(L, 1, N))


def _ffn_kernel(x_ref, xn_ref, mod_ref, modn_ref, g_ref, wgu_hbm, wo_hbm, o_ref,
                h_ref, hn_ref, gu_buf, wo_buf, sem, *, row0):
    i = pl.program_id(0)
    n_tiles = pl.num_programs(0)
    nf = wgu_hbm.shape[0]

    def chunk_copies(j, slot):
        return (
            pltpu.make_async_copy(wgu_hbm.at[j], gu_buf.at[slot], sem.at[slot, 0]),
            pltpu.make_async_copy(wo_hbm.at[j], wo_buf.at[slot], sem.at[slot, 1]),
        )

    def normed(xs, ms):
        return _norm_modulate(xs, g_ref[...], ms[row0:row0 + 1, :], ms[row0 + 1:row0 + 2, :]).astype(BF16)

    @pl.when(i == 0)
    def _():
        for cp in chunk_copies(0, 0):
            cp.start()
        h_ref[...] = normed(x_ref[...], mod_ref)

    @pl.when(i > 0)
    def _():
        h_ref[...] = hn_ref[...]

    o_ref[...] = jnp.zeros_like(o_ref)
    parts = FFN_NORM_PARTS if nf >= FFN_NORM_PARTS else 1
    rows = x_ref.shape[0] // parts

    def body(j, carry):
        slot = (i * nf + j) % 2

        @pl.when(jnp.logical_or(j + 1 < nf, i + 1 < n_tiles))
        def _():
            for cp in chunk_copies(jnp.where(j + 1 < nf, j + 1, 0), 1 - slot):
                cp.start()

        for cp in chunk_copies(j, slot):
            cp.wait()
        r0 = pl.multiple_of(jnp.minimum(j, parts - 1) * rows, rows)
        hn_ref[pl.ds(r0, rows), :] = normed(xn_ref[pl.ds(r0, rows), :], modn_ref)
        hh = h_ref[...]
        a_g = jnp.dot(hh, gu_buf[slot, 0], preferred_element_type=F32)
        a_u = jnp.dot(hh, gu_buf[slot, 1], preferred_element_type=F32)
        act = (a_g * jax.nn.sigmoid(a_g) * a_u).astype(BF16)
        o_ref[...] += jnp.dot(act, wo_buf[slot], preferred_element_type=F32)
        return carry

    lax.fori_loop(0, nf, body, 0)
    o_ref[...] = x_ref[...] + 0.5 * mod_ref[row0 + 2:row0 + 3, :] * o_ref[...]


def _ffn_weights(w_in, w_out, tf):
    D = w_in.shape[0]
    F = w_out.shape[0]
    nf = F // tf
    w_gu = w_in.reshape(D, 2, nf, tf).transpose(2, 1, 0, 3).astype(BF16)
    return w_gu, w_out.reshape(nf, tf, D).astype(BF16)


def _ffn(x, mod, g, w_gu, w_o, *, row0, seq, tm):
    T, D = x.shape
    nf, tf, _ = w_o.shape
    tpb = seq // tm
    n_tiles = T // tm
    return pl.pallas_call(
        functools.partial(_ffn_kernel, row0=row0),
        grid=(n_tiles,),
        in_specs=[
            pl.BlockSpec((tm, D), lambda i: (i, 0)),
            pl.BlockSpec((tm, D), lambda i: (jnp.minimum(i + 1, n_tiles - 1), 0)),
            pl.BlockSpec((None, N_MOD, D), lambda i: (i // tpb, 0, 0)),
            pl.BlockSpec((None, N_MOD, D), lambda i: (jnp.minimum(i + 1, n_tiles - 1) // tpb, 0, 0)),
            pl.BlockSpec((1, D), lambda i: (0, 0)),
            pl.BlockSpec(memory_space=pl.ANY),
            pl.BlockSpec(memory_space=pl.ANY),
        ],
        out_specs=pl.BlockSpec((tm, D), lambda i: (i, 0)),
        out_shape=jax.ShapeDtypeStruct((T, D), F32),
        scratch_shapes=[
            pltpu.VMEM((tm, D), BF16),
            pltpu.VMEM((tm, D), BF16),
            pltpu.VMEM((2, 2, D, tf), BF16),
            pltpu.VMEM((2, tf, D), BF16),
            pltpu.SemaphoreType.DMA((2, 2)),
        ],
        compiler_params=_params(("arbitrary",)),
        name="ffn",
    )(x, x, mod, mod, g, w_gu, w_o)


def _mix_in_kernel(x_ref, mod_ref, g_ref, w_ref, wvt_ref, cos_ref, sin_ref, pw_ref, pb_ref, ps_ref,
                   q_ref, k_ref, vt_ref, yp_ref, pext_ref, *, tpb, width, n_heads):
    i = pl.program_id(0)
    tm = x_ref.shape[0]
    h = _norm_modulate(x_ref[...], g_ref[...], mod_ref[3:4, :], mod_ref[4:5, :]).astype(BF16)

    p = jnp.dot(h, w_ref[:, 0:width], preferred_element_type=F32)

    @pl.when(i % tpb == 0)
    def _():
        pext_ref[0:POOL_HALO, :] = jnp.zeros((POOL_HALO, width), F32)

    pext_ref[POOL_HALO:POOL_HALO + tm, :] = p
    pos = (i % tpb) * tm + lax.broadcasted_iota(jnp.int32, (tm, 1), 0)
    cg = width // len(POOL_WINDOWS)
    for gi, win in enumerate(POOL_WINDOWS):
        lo, hi = gi * cg, (gi + 1) * cg
        pg = p[:, lo:hi]
        acc = pg
        for back in range(1, win):
            acc = acc + pext_ref[POOL_HALO - back:POOL_HALO - back + tm, lo:hi]
        count = jnp.minimum(pos + 1, win).astype(F32)
        pooled = acc / count - pg
        mixed = jnp.dot(pooled.astype(BF16), pw_ref[gi], preferred_element_type=F32) + pb_ref[:, lo:hi]
        yp_ref[:, lo:hi] = (mixed * ps_ref[:, lo:hi]).astype(BF16)
    pext_ref[0:POOL_HALO, :] = pext_ref[tm:tm + POOL_HALO, :]

    cos = cos_ref[...]
    sin = sin_ref[...]
    lane = lax.broadcasted_iota(jnp.int32, (tm, HEAD_V), 1)
    first_half = (lane % HEAD_DIM) < (HEAD_DIM // 2)
    for sec, out_ref, mul in ((1, q_ref, Q_SCALE), (2, k_ref, None)):
        z = jnp.dot(h, w_ref[:, sec * width:(sec + 1) * width], preferred_element_type=F32)
        for hh in range(n_heads):
            zh = z[:, hh * HEAD_V:(hh + 1) * HEAD_V]
            partner = jnp.where(first_half,
                                pltpu.roll(zh, HEAD_V - HEAD_DIM // 2, axis=1),
                                pltpu.roll(zh, HEAD_DIM // 2, axis=1))
            r = zh * cos + partner * sin
            if mul is not None:
                r = r * mul
            out_ref[hh] = r.astype(BF16)

    tk = vt_ref.shape[3]
    zt = lax.dot_general(wvt_ref[...], h, (((1,), (1,)), ((), ())), preferred_element_type=F32)
    ones_rows = (lax.broadcasted_iota(jnp.int32, (ONES_ROWS, tk), 0) == 0).astype(BF16)
    for hh in range(n_heads):
        for ck in range(tm // tk):
            vt_ref[hh, ck, 0:HEAD_V, :] = zt[hh * HEAD_V:(hh + 1) * HEAD_V, ck * tk:(ck + 1) * tk].astype(BF16)
            vt_ref[hh, ck, HEAD_V:VT_ROWS, :] = ones_rows


def _mix_in(x, mod, g, w_in, w_vt, cos, sin, pool_w, pool_b, pool_scale, *, seq, tm, tk):
    T, D = x.shape
    width = w_vt.shape[0]
    n_heads = width // HEAD_V
    tpb = seq // tm
    head_major = jax.ShapeDtypeStruct((n_heads, T, HEAD_V), BF16)
    head_spec = pl.BlockSpec((n_heads, tm, HEAD_V), lambda i: (0, i, 0))
    vt_shape = jax.ShapeDtypeStruct((n_heads, T // tk, VT_ROWS, tk), BF16)
    vt_spec = pl.BlockSpec((n_heads, tm // tk, VT_ROWS, tk), lambda i: (0, i, 0, 0))
    return pl.pallas_call(
        functools.partial(_mix_in_kernel, tpb=tpb, width=width, n_heads=n_heads),
        grid=(T // tm,),
        in_specs=[
            pl.BlockSpec((tm, D), lambda i: (i, 0)),
            pl.BlockSpec((None, N_MOD, D), lambda i: (i // tpb, 0, 0)),
            pl.BlockSpec((1, D), lambda i: (0, 0)),
            _resident((D, 3 * width), lambda i: (0, 0)),
            _resident((width, D), lambda i: (0, 0)),
            pl.BlockSpec((tm, HEAD_V), lambda i: (i % tpb, 0)),
            pl.BlockSpec((tm, HEAD_V), lambda i: (i % tpb, 0)),
            _resident(pool_w.shape, lambda i: (0, 0, 0)),
            pl.BlockSpec((1, width), lambda i: (0, 0)),
            pl.BlockSpec((1, width), lambda i: (0, 0)),
        ],
        out_specs=[head_spec, head_spec, vt_spec, pl.BlockSpec((tm, width), lambda i: (i, 0))],
        out_shape=[head_major, head_major, vt_shape, jax.ShapeDtypeStruct((T, width), BF16)],
        scratch_shapes=[pltpu.VMEM((POOL_HALO + tm, width), F32)],
        compiler_params=_params(("arbitrary",)),
        name="mix_in",
    )(x, mod, g, w_in, w_vt, cos, sin, pool_w, pool_b, pool_scale)


def _attn_kernel(linit_ref, q_ref, k_ref, vt_ref, dl_ref, g_ref, o_ref,
                 qs_ref, sa_ref, sb_ref, mca_ref, mcb_ref, m_ref, acc_ref, *, tk):
    i = pl.program_id(2)
    tq = q_ref.shape[0]
    vc = vt_ref.shape[2]

    q = q_ref[...]
    lane = lax.broadcasted_iota(jnp.int32, q.shape, 1)
    qs_ref[0:tq, :] = jnp.where(lane < HEAD_DIM, q, jnp.zeros_like(q))
    qs_ref[tq:2 * tq, :] = jnp.where(lane < HEAD_DIM, jnp.zeros_like(q), q)
    m_ref[...] = jnp.full(m_ref.shape, MASK_VALUE, F32)
    acc_ref[...] = jnp.zeros(acc_ref.shape, F32)

    last = (i * tq) // tk

    buf_a = (sa_ref, mca_ref)
    buf_b = (sb_ref, mcb_ref)

    sub = min(ATTN_SUB, vc)
    n_sub = tk // sub

    def score_sub(j, u, masked):
        start = pl.multiple_of(j * tk + u * sub, sub)
        s = lax.dot_general(k_ref[pl.ds(start, sub), :], qs_ref[...], (((1,), (1,)), ((), ())),
                            preferred_element_type=F32)
        if masked:
            key_pos = start + lax.broadcasted_iota(jnp.int32, s.shape, 0)
            col = lax.broadcasted_iota(jnp.int32, s.shape, 1)
            q_pos = i * tq + jnp.where(col >= tq, col - tq, col)
            s = jnp.where(key_pos <= q_pos, s, MASK_VALUE)
        return s

    def step(j, cur, nxt, next_masked):
        if cur is not None:
            s_cur, mc_cur = cur
            m_prev = m_ref[...]
            m_new = jnp.maximum(m_prev, mc_cur[...])
            alpha = jnp.exp2(m_prev - m_new)
        pv = None
        mx = None
        for u in range(n_sub):
            if nxt is not None:
                s = score_sub(j + 1 if cur is not None else j, u, next_masked)
                nxt[0][u * sub:(u + 1) * sub, :] = s
                smax = jnp.max(s, axis=0, keepdims=True)
                mx = smax if mx is None else jnp.maximum(mx, smax)
            if cur is not None:
                p = jnp.exp2(s_cur[u * sub:(u + 1) * sub, :] - m_new).astype(BF16)
                lo = (u * sub) % vc
                vt = vt_ref[j * (tk // vc) + (u * sub) // vc, :, lo:lo + sub]
                d = jnp.dot(vt, p, preferred_element_type=F32)
                pv = d if pv is None else pv + d
        if nxt is not None:
            nxt[1][...] = mx
        if cur is not None:
            acc_ref[...] = alpha * acc_ref[...] + pv
            m_ref[...] = m_new

    @pl.when(last == 0)
    def _():
        step(0, None, buf_a, True)

    @pl.when(last > 0)
    def _():
        step(0, None, buf_a, False)

    trips = jnp.maximum(last - 1, 0) // 2

    def body(t, carry):
        step(2 * t, buf_a, buf_b, False)
        step(2 * t + 1, buf_b, buf_a, False)
        return carry

    lax.fori_loop(0, trips, body, 0)
    done = 2 * trips

    @pl.when(last == done)
    def _():
        step(last, buf_a, None, None)

    @pl.when(last == done + 1)
    def _():
        step(done, buf_a, buf_b, True)
        step(last, buf_b, None, None)

    @pl.when(last == done + 2)
    def _():
        step(done, buf_a, buf_b, False)
        step(done + 1, buf_b, buf_a, True)
        step(last, buf_a, None, None)

    dl = dl_ref[...]
    lambda_init = linit_ref[0]
    lam = (jnp.exp(jnp.sum(dl[0:1] * dl[1:2], axis=1, keepdims=True))
           - jnp.exp(jnp.sum(dl[2:3] * dl[3:4], axis=1, keepdims=True)) + lambda_init)
    num = acc_ref[0:HEAD_V, :]
    den = acc_ref[HEAD_V:HEAD_V + 1, :]
    o = num[:, 0:tq] / den[:, 0:tq] - lam * (num[:, tq:2 * tq] / den[:, tq:2 * tq])
    o = o * lax.rsqrt(jnp.mean(o * o, axis=0, keepdims=True) + SUBLN_EPS)
    o_ref[...] = (o.T * g_ref[...] * (1.0 - lambda_init)).astype(BF16)


def _attention(linit, q, k, vt, diff_lambda, subln, *, batch, seq, tq, tk):
    n_heads, T, _ = q.shape
    vc = vt.shape[3]
    nq = seq // tq
    nk = seq // vc
    return pl.pallas_call(
        functools.partial(_attn_kernel, tk=tk),
        grid=(batch, n_heads, nq),
        in_specs=[
            pl.BlockSpec(memory_space=pltpu.SMEM),
            pl.BlockSpec((None, tq, HEAD_V), lambda b, h, i: (h, b * nq + i, 0)),
            pl.BlockSpec((None, seq, HEAD_V), lambda b, h, i: (h, b, 0)),
            pl.BlockSpec((None, nk, VT_ROWS, vc), lambda b, h, i: (h, b, 0, 0)),
            pl.BlockSpec(diff_lambda.shape, lambda b, h, i: (0, 0)),
            pl.BlockSpec((1, HEAD_V), lambda b, h, i: (0, 0)),
        ],
        out_specs=pl.BlockSpec((tq, HEAD_V), lambda b, h, i: (b * nq + i, h)),
        out_shape=jax.ShapeDtypeStruct((T, n_heads * HEAD_V), BF16),
        scratch_shapes=[
            pltpu.VMEM((2 * tq, HEAD_V), BF16),
            pltpu.VMEM((tk, 2 * tq), F32),
            pltpu.VMEM((tk, 2 * tq), F32),
            pltpu.VMEM((1, 2 * tq), F32),
            pltpu.VMEM((1, 2 * tq), F32),
            pltpu.VMEM((1, 2 * tq), F32),
            pltpu.VMEM((VT_ROWS, 2 * tq), F32),
        ],
        compiler_params=_params(("parallel", "parallel", "arbitrary")),
        name="diff_attn",
    )(linit, q, k, vt, diff_lambda, subln)


def _mix_out_kernel(x_ref, mod_ref, yp_ref, ya_ref, w_ref, o_ref):
    width = yp_ref.shape[1]
    y = jnp.dot(yp_ref[...], w_ref[0:width, :], preferred_element_type=F32)
    y = y + jnp.dot(ya_ref[...], w_ref[width:2 * width, :], preferred_element_type=F32)
    o_ref[...] = x_ref[...] + mod_ref[5:6, :] * y


def _mix_out(x, mod, y_pool, y_attn, w_out, *, seq, tm):
    T, D = x.shape
    width = y_pool.shape[1]
    tpb = seq // tm
    return pl.pallas_call(
        _mix_out_kernel,
        grid=(T // tm,),
        in_specs=[
            pl.BlockSpec((tm, D), lambda i: (i, 0)),
            pl.BlockSpec((None, N_MOD, D), lambda i: (i // tpb, 0, 0)),
            pl.BlockSpec((tm, width), lambda i: (i, 0)),
            pl.BlockSpec((tm, width), lambda i: (i, 0)),
            _resident(w_out.shape, lambda i: (0, 0)),
        ],
        out_specs=pl.BlockSpec((tm, D), lambda i: (i, 0)),
        out_shape=jax.ShapeDtypeStruct((T, D), F32),
        compiler_params=_params(("parallel",)),
        name="mix_out",
    )(x, mod, y_pool, y_attn, w_out)


def _final_norm_kernel(x_ref, g_ref, o_ref):
    x = x_ref[...]
    o_ref[...] = x * lax.rsqrt(jnp.mean(x * x, axis=-1, keepdims=True) + RMS_EPS) * g_ref[...]


def _final_norm(x, g, *, tm):
    T, D = x.shape
    return pl.pallas_call(
        _final_norm_kernel,
        grid=(T // tm,),
        in_specs=[pl.BlockSpec((tm, D), lambda i: (i, 0)), pl.BlockSpec((1, D), lambda i: (0, 0))],
        out_specs=pl.BlockSpec((tm, D), lambda i: (i, 0)),
        out_shape=jax.ShapeDtypeStruct((T, D), F32),
        compiler_params=_params(("parallel",)),
        name="final_norm",
    )(x, g)


def _rope_tables(seq):
    pos = jnp.arange(seq, dtype=F32)
    inv_freq = ROPE_THETA ** (-jnp.arange(0, HEAD_DIM, 2, dtype=F32) / HEAD_DIM)
    ang = pos[:, None] * inv_freq[None, :]
    c, s = jnp.cos(ang), jnp.sin(ang)
    return jnp.concatenate([c, c, c, c], axis=1), jnp.concatenate([-s, s, -s, s], axis=1)


def kernel(x, c, w_mod, b_mod, norm_ffn1, ffn1_w_in, ffn1_w_out, norm_mix, w_in, pool_w, pool_b, pool_scale,
           diff_lambda, diff_subln, w_out, norm_ffn2, ffn2_w_in, ffn2_w_out, final_norm):
    B, S, D = x.shape
    L = w_mod.shape[0]
    F = ffn1_w_out.shape[1]
    tm = min(TOKEN_TILE, S)
    tq = min(Q_TILE, S)
    tk = min(KV_CHUNK, S)
    vc = min(VT_CHUNK, tk, tm)
    tf = FF_CHUNK if F % FF_CHUNK == 0 else F
    assert S % tm == 0 and S % tk == 0 and tm % vc == 0 and tk % vc == 0 and tk % tq == 0
    width = pool_scale.shape[1]

    cos, sin = _rope_tables(S)
    mod_all = _modulation(c, w_mod, b_mod).reshape(L, B, N_MOD, D)
    xt = x.reshape(B * S, D)
    for l in range(L):
        mod = mod_all[l]
        linit = jnp.full((1,), 0.8 - 0.6 * math.exp(-0.3 * l), F32)
        xt = _ffn(xt, mod, norm_ffn1[l].reshape(1, D), *_ffn_weights(ffn1_w_in[l], ffn1_w_out[l], tf),
                  row0=0, seq=S, tm=tm)
        w_pqk = w_in[l, :, :3 * width].astype(BF16)
        w_vt = w_in[l, :, 3 * width:].T.astype(BF16)
        q, k, vt, y_pool = _mix_in(xt, mod, norm_mix[l].reshape(1, D), w_pqk, w_vt, cos, sin,
                                   pool_w[l].astype(BF16), pool_b[l].reshape(1, width),
                                   pool_scale[l].reshape(1, width), seq=S, tm=tm, tk=vc)
        y_attn = _attention(linit, q, k, vt, diff_lambda[l], diff_subln[l].reshape(1, HEAD_V),
                            batch=B, seq=S, tq=tq, tk=tk)
        xt = _mix_out(xt, mod, y_pool, y_attn, w_out[l].astype(BF16), seq=S, tm=tm)
        xt = _ffn(xt, mod, norm_ffn2[l].reshape(1, D), *_ffn_weights(ffn2_w_in[l], ffn2_w_out[l], tf),
                  row0=6, seq=S, tm=tm)
    return _final_norm(xt, final_norm.reshape(1, D), tm=tm).reshape(B, S, D)
```

```python
import functools
import math

import jax
import jax.numpy as jnp
from jax import lax
from jax.experimental import pallas as pl
from jax.experimental.pallas import tpu as pltpu

F32 = jnp.float32
BF16 = jnp.bfloat16

POOL_WINDOWS = (2, 4, 8, 16)
POOL_HALO = 16
HEAD_DIM = 64
HEAD_V = 128
ROPE_THETA = 10000.0
RMS_EPS = 1e-6
SUBLN_EPS = 1e-5
N_MOD = 9
MASK_VALUE = -1e30
Q_SCALE = HEAD_DIM ** -0.5 * math.log2(math.e)
ONES_ROWS = 16
VT_ROWS = HEAD_V + ONES_ROWS

VMEM_LIMIT_BYTES = 56 * 1024 * 1024

TOKEN_TILE = 512
FF_CHUNK = 512
FFN_SLOTS = 3
FFN_NORM_PARTS = 8
Q_TILE = 512
KV_CHUNK = 1024
VT_CHUNK = 512
ATTN_SUB = 256


def _params(sem):
    return pltpu.CompilerParams(dimension_semantics=sem, vmem_limit_bytes=VMEM_LIMIT_BYTES)


def _resident(block_shape, index_map):
    return pl.BlockSpec(block_shape, index_map, pipeline_mode=pl.Buffered(1))


def _norm_modulate(x, g, shift, scale):
    y = x * lax.rsqrt(jnp.mean(x * x, axis=-1, keepdims=True) + RMS_EPS) * g
    return y * (1.0 + scale) + shift


def _mod_kernel(ct_ref, w_ref, b_ref, o_ref):
    ct = ct_ref[...]
    ct = ct * jax.nn.sigmoid(ct)
    w = w_ref[...]
    rows = [jnp.sum(w * ct[:, b:b + 1], axis=0, keepdims=True) for b in range(ct.shape[1])]
    o_ref[...] = jnp.concatenate(rows, axis=0) + b_ref[...]


def _modulation(c, w_mod, b_mod, tn=1024):
    L, D, N = w_mod.shape
    B = c.shape[0]
    return pl.pallas_call(
        _mod_kernel,
        grid=(L, N // tn),
        in_specs=[
            pl.BlockSpec((D, B), lambda l, n: (0, 0)),
            pl.BlockSpec((None, D, tn), lambda l, n: (l, 0, n)),
            pl.BlockSpec((None, 1, tn), lambda l, n: (l, 0, n)),
        ],
        out_specs=pl.BlockSpec((None, B, tn), lambda l, n: (l, 0, n)),
        out_shape=jax.ShapeDtypeStruct((L, B, N), F32),
        compiler_params=_params(("parallel", "parallel")),
        name="modulation",
    )(c.T, w_mod, b_mod.reshape(L, 1, N))


def _ffn_kernel(x_ref, xn_ref, mod_ref, modn_ref, g_ref, wgu_hbm, wo_hbm, o_ref,
                h_ref, hn_ref, gu_buf, wo_buf, sem, *, row0):
    i = pl.program_id(0)
    n_tiles = pl.num_programs(0)
    nf = wgu_hbm.shape[0]

    def chunk_copies(j, slot):
        return (
            pltpu.make_async_copy(wgu_hbm.at[j], gu_buf.at[slot], sem.at[slot, 0]),
            pltpu.make_async_copy(wo_hbm.at[j], wo_buf.at[slot], sem.at[slot, 1]),
        )

    def normed(xs, ms):
        return _norm_modulate(xs, g_ref[...], ms[row0:row0 + 1, :], ms[row0 + 1:row0 + 2, :]).astype(BF16)

    @pl.when(i == 0)
    def _():
        for ahead in range(FFN_SLOTS - 1):
            for cp in chunk_copies(ahead, ahead):
                cp.start()
        h_ref[...] = normed(x_ref[...], mod_ref)

    @pl.when(i > 0)
    def _():
        h_ref[...] = hn_ref[...]

    o_ref[...] = jnp.zeros_like(o_ref)
    parts = FFN_NORM_PARTS if nf >= FFN_NORM_PARTS else 1
    rows = x_ref.shape[0] // parts

    def body(j, carry):
        gstep = i * nf + j
        slot = gstep % FFN_SLOTS
        ahead = FFN_SLOTS - 1

        @pl.when(gstep + ahead < n_tiles * nf)
        def _():
            for cp in chunk_copies((j + ahead) % nf, (gstep + ahead) % FFN_SLOTS):
                cp.start()

        for cp in chunk_copies(j, slot):
            cp.wait()
        r0 = pl.multiple_of(jnp.minimum(j, parts - 1) * rows, rows)
        hn_ref[pl.ds(r0, rows), :] = normed(xn_ref[pl.ds(r0, rows), :], modn_ref)
        hh = h_ref[...]
        a_g = jnp.dot(hh, gu_buf[slot, 0], preferred_element_type=F32)
        a_u = jnp.dot(hh, gu_buf[slot, 1], preferred_element_type=F32)
        act = (a_g * jax.nn.sigmoid(a_g) * a_u).astype(BF16)
        o_ref[...] += jnp.dot(act, wo_buf[slot], preferred_element_type=F32)
        return carry

    lax.fori_loop(0, nf, body, 0)
    o_ref[...] = x_ref[...] + 0.5 * mod_ref[row0 + 2:row0 + 3, :] * o_ref[...]


def _ffn_weights(w_in, w_out, tf):
    D = w_in.shape[0]
    F = w_out.shape[0]
    nf = F // tf
    w_gu = w_in.reshape(D, 2, nf, tf).transpose(2, 1, 0, 3).astype(BF16)
    return w_gu, w_out.reshape(nf, tf, D).astype(BF16)


def _ffn(x, mod, g, w_gu, w_o, *, row0, seq, tm):
    T, D = x.shape
    nf, tf, _ = w_o.shape
    tpb = seq // tm
    n_tiles = T // tm
    return pl.pallas_call(
        functools.partial(_ffn_kernel, row0=row0),
        grid=(n_tiles,),
        in_specs=[
            pl.BlockSpec((tm, D), lambda i: (i, 0)),
            pl.BlockSpec((tm, D), lambda i: (jnp.minimum(i + 1, n_tiles - 1), 0)),
            pl.BlockSpec((None, N_MOD, D), lambda i: (i // tpb, 0, 0)),
            pl.BlockSpec((None, N_MOD, D), lambda i: (jnp.minimum(i + 1, n_tiles - 1) // tpb, 0, 0)),
            pl.BlockSpec((1, D), lambda i: (0, 0)),
            pl.BlockSpec(memory_space=pl.ANY),
            pl.BlockSpec(memory_space=pl.ANY),
        ],
        out_specs=pl.BlockSpec((tm, D), lambda i: (i, 0)),
        out_shape=jax.ShapeDtypeStruct((T, D), F32),
        scratch_shapes=[
            pltpu.VMEM((tm, D), BF16),
            pltpu.VMEM((tm, D), BF16),
            pltpu.VMEM((FFN_SLOTS, 2, D, tf), BF16),
            pltpu.VMEM((FFN_SLOTS, tf, D), BF16),
            pltpu.SemaphoreType.DMA((FFN_SLOTS, 2)),
        ],
        compiler_params=_params(("arbitrary",)),
        name="ffn",
    )(x, x, mod, mod, g, w_gu, w_o)


def _mix_in_kernel(x_ref, mod_ref, g_ref, w_ref, wvt_ref, cos_ref, sin_ref, pw_ref, pb_ref, ps_ref,
                   q_ref, k_ref, vt_ref, yp_ref, pext_ref, *, tpb, width, n_heads):
    i = pl.program_id(0)
    tm = x_ref.shape[0]
    h = _norm_modulate(x_ref[...], g_ref[...], mod_ref[3:4, :], mod_ref[4:5, :]).astype(BF16)

    p = jnp.dot(h, w_ref[:, 0:width], preferred_element_type=F32)

    @pl.when(i % tpb == 0)
    def _():
        pext_ref[0:POOL_HALO, :] = jnp.zeros((POOL_HALO, width), F32)

    pext_ref[POOL_HALO:POOL_HALO + tm, :] = p
    pos = (i % tpb) * tm + lax.broadcasted_iota(jnp.int32, (tm, 1), 0)
    cg = width // len(POOL_WINDOWS)
    for gi, win in enumerate(POOL_WINDOWS):
        lo, hi = gi * cg, (gi + 1) * cg
        pg = p[:, lo:hi]
        acc = pg
        for back in range(1, win):
            acc = acc + pext_ref[POOL_HALO - back:POOL_HALO - back + tm, lo:hi]
        count = jnp.minimum(pos + 1, win).astype(F32)
        pooled = acc / count - pg
        mixed = jnp.dot(pooled.astype(BF16), pw_ref[gi], preferred_element_type=F32) + pb_ref[:, lo:hi]
        yp_ref[:, lo:hi] = (mixed * ps_ref[:, lo:hi]).astype(BF16)
    pext_ref[0:POOL_HALO, :] = pext_ref[tm:tm + POOL_HALO, :]

    cos = cos_ref[...]
    sin = sin_ref[...]
    lane = lax.broadcasted_iota(jnp.int32, (tm, HEAD_V), 1)
    first_half = (lane % HEAD_DIM) < (HEAD_DIM // 2)
    for sec, out_ref, mul in ((1, q_ref, Q_SCALE), (2, k_ref, None)):
        z = jnp.dot(h, w_ref[:, sec * width:(sec + 1) * width], preferred_element_type=F32)
        for hh in range(n_heads):
            zh = z[:, hh * HEAD_V:(hh + 1) * HEAD_V]
            partner = jnp.where(first_half,
                                pltpu.roll(zh, HEAD_V - HEAD_DIM // 2, axis=1),
                                pltpu.roll(zh, HEAD_DIM // 2, axis=1))
            r = zh * cos + partner * sin
            if mul is not None:
                r = r * mul
            out_ref[hh] = r.astype(BF16)

    tk = vt_ref.shape[3]
    zt = lax.dot_general(wvt_ref[...], h, (((1,), (1,)), ((), ())), preferred_element_type=F32)
    ones_rows = (lax.broadcasted_iota(jnp.int32, (ONES_ROWS, tk), 0) == 0).astype(BF16)
    for hh in range(n_heads):
        for ck in range(tm // tk):
            vt_ref[hh, ck, 0:HEAD_V, :] = zt[hh * HEAD_V:(hh + 1) * HEAD_V, ck * tk:(ck + 1) * tk].astype(BF16)
            vt_ref[hh, ck, HEAD_V:VT_ROWS, :] = ones_rows


def _mix_in(x, mod, g, w_in, w_vt, cos, sin, pool_w, pool_b, pool_scale, *, seq, tm, tk):
    T, D = x.shape
    width = w_vt.shape[0]
    n_heads = width // HEAD_V
    tpb = seq // tm
    head_major = jax.ShapeDtypeStruct((n_heads, T, HEAD_V), BF16)
    head_spec = pl.BlockSpec((n_heads, tm, HEAD_V), lambda i: (0, i, 0))
    vt_shape = jax.ShapeDtypeStruct((n_heads, T // tk, VT_ROWS, tk), BF16)
    vt_spec = pl.BlockSpec((n_heads, tm // tk, VT_ROWS, tk), lambda i: (0, i, 0, 0))
    return pl.pallas_call(
        functools.partial(_mix_in_kernel, tpb=tpb, width=width, n_heads=n_heads),
        grid=(T // tm,),
        in_specs=[
            pl.BlockSpec((tm, D), lambda i: (i, 0)),
            pl.BlockSpec((None, N_MOD, D), lambda i: (i // tpb, 0, 0)),
            pl.BlockSpec((1, D), lambda i: (0, 0)),
            _resident((D, 3 * width), lambda i: (0, 0)),
            _resident((width, D), lambda i: (0, 0)),
            pl.BlockSpec((tm, HEAD_V), lambda i: (i % tpb, 0)),
            pl.BlockSpec((tm, HEAD_V), lambda i: (i % tpb, 0)),
            _resident(pool_w.shape, lambda i: (0, 0, 0)),
            pl.BlockSpec((1, width), lambda i: (0, 0)),
            pl.BlockSpec((1, width), lambda i: (0, 0)),
        ],
        out_specs=[head_spec, head_spec, vt_spec, pl.BlockSpec((tm, width), lambda i: (i, 0))],
        out_shape=[head_major, head_major, vt_shape, jax.ShapeDtypeStruct((T, width), BF16)],
        scratch_shapes=[pltpu.VMEM((POOL_HALO + tm, width), F32)],
        compiler_params=_params(("arbitrary",)),
        name="mix_in",
    )(x, mod, g, w_in, w_vt, cos, sin, pool_w, pool_b, pool_scale)


def _attn_kernel(linit_ref, q_ref, k_ref, vt_ref, dl_ref, g_ref, o_ref,
                 qs_ref, sa_ref, sb_ref, mca_ref, mcb_ref, m_ref, acc_ref, *, tk):
    i = pl.program_id(2)
    tq = q_ref.shape[0]
    vc = vt_ref.shape[2]

    q = q_ref[...]
    lane = lax.broadcasted_iota(jnp.int32, q.shape, 1)
    qs_ref[0:tq, :] = jnp.where(lane < HEAD_DIM, q, jnp.zeros_like(q))
    qs_ref[tq:2 * tq, :] = jnp.where(lane < HEAD_DIM, jnp.zeros_like(q), q)
    m_ref[...] = jnp.full(m_ref.shape, MASK_VALUE, F32)
    acc_ref[...] = jnp.zeros(acc_ref.shape, F32)

    last = (i * tq) // tk

    buf_a = (sa_ref, mca_ref)
    buf_b = (sb_ref, mcb_ref)

    sub = min(ATTN_SUB, vc)
    n_sub = tk // sub

    def score_sub(j, u, masked):
        start = pl.multiple_of(j * tk + u * sub, sub)
        s = lax.dot_general(k_ref[pl.ds(start, sub), :], qs_ref[...], (((1,), (1,)), ((), ())),
                            preferred_element_type=F32)
        if masked:
            key_pos = start + lax.broadcasted_iota(jnp.int32, s.shape, 0)
            col = lax.broadcasted_iota(jnp.int32, s.shape, 1)
            q_pos = i * tq + jnp.where(col >= tq, col - tq, col)
            s = jnp.where(key_pos <= q_pos, s, MASK_VALUE)
        return s

    def step(j, cur, nxt, next_masked):
        if cur is not None:
            s_cur, mc_cur = cur
            m_prev = m_ref[...]
            m_new = jnp.maximum(m_prev, mc_cur[...])
            alpha = jnp.exp2(m_prev - m_new)
        pv = None
        mx = None
        for u in range(n_sub):
            if nxt is not None:
                s = score_sub(j + 1 if cur is not None else j, u, next_masked)
                nxt[0][u * sub:(u + 1) * sub, :] = s
                smax = jnp.max(s, axis=0, keepdims=True)
                mx = smax if mx is None else jnp.maximum(mx, smax)
            if cur is not None:
                p = jnp.exp2(s_cur[u * sub:(u + 1) * sub, :] - m_new).astype(BF16)
                lo = (u * sub) % vc
                vt = vt_ref[j * (tk // vc) + (u * sub) // vc, :, lo:lo + sub]
                d = jnp.dot(vt, p, preferred_element_type=F32)
                pv = d if pv is None else pv + d
        if nxt is not None:
            nxt[1][...] = mx
        if cur is not None:
            acc_ref[...] = alpha * acc_ref[...] + pv
            m_ref[...] = m_new

    @pl.when(last == 0)
    def _():
        step(0, None, buf_a, True)

    @pl.when(last > 0)
    def _():
        step(0, None, buf_a, False)

    trips = jnp.maximum(last - 1, 0) // 2

    def body(t, carry):
        step(2 * t, buf_a, buf_b, False)
        step(2 * t + 1, buf_b, buf_a, False)
        return carry

    lax.fori_loop(0, trips, body, 0)
    done = 2 * trips

    @pl.when(last == done)
    def _():
        step(last, buf_a, None, None)

    @pl.when(last == done + 1)
    def _():
        step(done, buf_a, buf_b, True)
        step(last, buf_b, None, None)

    @pl.when(last == done + 2)
    def _():
        step(done, buf_a, buf_b, False)
        step(done + 1, buf_b, buf_a, True)
        step(last, buf_a, None, None)

    dl = dl_ref[...]
    lambda_init = linit_ref[0]
    lam = (jnp.exp(jnp.sum(dl[0:1] * dl[1:2], axis=1, keepdims=True))
           - jnp.exp(jnp.sum(dl[2:3] * dl[3:4], axis=1, keepdims=True)) + lambda_init)
    num = acc_ref[0:HEAD_V, :]
    den = acc_ref[HEAD_V:HEAD_V + 1, :]
    inv = 1.0 / den
    o = num[:, 0:tq] * inv[:, 0:tq] - lam * (num[:, tq:2 * tq] * inv[:, tq:2 * tq])
    o = o * lax.rsqrt(jnp.mean(o * o, axis=0, keepdims=True) + SUBLN_EPS)
    o_ref[...] = (o.T * g_ref[...] * (1.0 - lambda_init)).astype(BF16)


def _attention(linit, q, k, vt, diff_lambda, subln, *, batch, seq, tq, tk):
    n_heads, T, _ = q.shape
    vc = vt.shape[3]
    nq = seq // tq
    nk = seq // vc
    return pl.pallas_call(
        functools.partial(_attn_kernel, tk=tk),
        grid=(batch, n_heads, nq),
        in_specs=[
            pl.BlockSpec(memory_space=pltpu.SMEM),
            pl.BlockSpec((None, tq, HEAD_V), lambda b, h, i: (h, b * nq + i, 0)),
            pl.BlockSpec((None, seq, HEAD_V), lambda b, h, i: (h, b, 0)),
            pl.BlockSpec((None, nk, VT_ROWS, vc), lambda b, h, i: (h, b, 0, 0)),
            pl.BlockSpec(diff_lambda.shape, lambda b, h, i: (0, 0)),
            pl.BlockSpec((1, HEAD_V), lambda b, h, i: (0, 0)),
        ],
        out_specs=pl.BlockSpec((tq, HEAD_V), lambda b, h, i: (b * nq + i, h)),
        out_shape=jax.ShapeDtypeStruct((T, n_heads * HEAD_V), BF16),
        scratch_shapes=[
            pltpu.VMEM((2 * tq, HEAD_V), BF16),
            pltpu.VMEM((tk, 2 * tq), F32),
            pltpu.VMEM((tk, 2 * tq), F32),
            pltpu.VMEM((1, 2 * tq), F32),
            pltpu.VMEM((1, 2 * tq), F32),
            pltpu.VMEM((1, 2 * tq), F32),
            pltpu.VMEM((VT_ROWS, 2 * tq), F32),
        ],
        compiler_params=_params(("parallel", "parallel", "arbitrary")),
        name="diff_attn",
    )(linit, q, k, vt, diff_lambda, subln)


def _mix_out_kernel(x_ref, mod_ref, yp_ref, ya_ref, w_ref, o_ref):
    width = yp_ref.shape[1]
    y = jnp.dot(yp_ref[...], w_ref[0:width, :], preferred_element_type=F32)
    y = y + jnp.dot(ya_ref[...], w_ref[width:2 * width, :], preferred_element_type=F32)
    o_ref[...] = x_ref[...] + mod_ref[5:6, :] * y


def _mix_out(x, mod, y_pool, y_attn, w_out, *, seq, tm):
    T, D = x.shape
    width = y_pool.shape[1]
    tpb = seq // tm
    return pl.pallas_call(
        _mix_out_kernel,
        grid=(T // tm,),
        in_specs=[
            pl.BlockSpec((tm, D), lambda i: (i, 0)),
            pl.BlockSpec((None, N_MOD, D), lambda i: (i // tpb, 0, 0)),
            pl.BlockSpec((tm, width), lambda i: (i, 0)),
            pl.BlockSpec((tm, width), lambda i: (i, 0)),
            _resident(w_out.shape, lambda i: (0, 0)),
        ],
        out_specs=pl.BlockSpec((tm, D), lambda i: (i, 0)),
        out_shape=jax.ShapeDtypeStruct((T, D), F32),
        compiler_params=_params(("parallel",)),
        name="mix_out",
    )(x, mod, y_pool, y_attn, w_out)


def _final_norm_kernel(x_ref, g_ref, o_ref):
    x = x_ref[...]
    o_ref[...] = x * lax.rsqrt(jnp.mean(x * x, axis=-1, keepdims=True) + RMS_EPS) * g_ref[...]


def _final_norm(x, g, *, tm):
    T, D = x.shape
    return pl.pallas_call(
        _final_norm_kernel,
        grid=(T // tm,),
        in_specs=[pl.BlockSpec((tm, D), lambda i: (i, 0)), pl.BlockSpec((1, D), lambda i: (0, 0))],
        out_specs=pl.BlockSpec((tm, D), lambda i: (i, 0)),
        out_shape=jax.ShapeDtypeStruct((T, D), F32),
        compiler_params=_params(("parallel",)),
        name="final_norm",
    )(x, g)


def _rope_tables(seq):
    pos = jnp.arange(seq, dtype=F32)
    inv_freq = ROPE_THETA ** (-jnp.arange(0, HEAD_DIM, 2, dtype=F32) / HEAD_DIM)
    ang = pos[:, None] * inv_freq[None, :]
    c, s = jnp.cos(ang), jnp.sin(ang)
    return jnp.concatenate([c, c, c, c], axis=1), jnp.concatenate([-s, s, -s, s], axis=1)


def kernel(x, c, w_mod, b_mod, norm_ffn1, ffn1_w_in, ffn1_w_out, norm_mix, w_in, pool_w, pool_b, pool_scale,
           diff_lambda, diff_subln, w_out, norm_ffn2, ffn2_w_in, ffn2_w_out, final_norm):
    B, S, D = x.shape
    L = w_mod.shape[0]
    F = ffn1_w_out.shape[1]
    tm = min(TOKEN_TILE, S)
    tq = min(Q_TILE, S)
    tk = min(KV_CHUNK, S)
    vc = min(VT_CHUNK, tk, tm)
    tf = FF_CHUNK if F % FF_CHUNK == 0 else F
    assert S % tm == 0 and S % tk == 0 and tm % vc == 0 and tk % vc == 0 and tk % tq == 0
    width = pool_scale.shape[1]

    cos, sin = _rope_tables(S)
    mod_all = _modulation(c, w_mod, b_mod).reshape(L, B, N_MOD, D)
    xt = x.reshape(B * S, D)
    for l in range(L):
        mod = mod_all[l]
        linit = jnp.full((1,), 0.8 - 0.6 * math.exp(-0.3 * l), F32)
        xt = _ffn(xt, mod, norm_ffn1[l].reshape(1, D), *_ffn_weights(ffn1_w_in[l], ffn1_w_out[l], tf),
                  row0=0, seq=S, tm=tm)
        w_pqk = w_in[l, :, :3 * width].astype(BF16)
        w_vt = w_in[l, :, 3 * width:].T.astype(BF16)
        q, k, vt, y_pool = _mix_in(xt, mod, norm_mix[l].reshape(1, D), w_pqk, w_vt, cos, sin,
                                   pool_w[l].astype(BF16), pool_b[l].reshape(1, width),
                                   pool_scale[l].reshape(1, width), seq=S, tm=tm, tk=vc)
        y_attn = _attention(linit, q, k, vt, diff_lambda[l], diff_subln[l].reshape(1, HEAD_V),
                            batch=B, seq=S, tq=tq, tk=tk)
        xt = _mix_out(xt, mod, y_pool, y_attn, w_out[l].astype(BF16), seq=S, tm=tm)
        xt = _ffn(xt, mod, norm_ffn2[l].reshape(1, D), *_ffn_weights(ffn2_w_in[l], ffn2_w_out[l], tf),
                  row0=6, seq=S, tm=tm)
    return _final_norm(xt, final_norm.reshape(1, D), tm=tm).reshape(B, S, D)
```

```python
import functools
import math

import jax
import jax.numpy as jnp
from jax import lax
from jax.experimental import pallas as pl
from jax.experimental.pallas import tpu as pltpu

F32 = jnp.float32
BF16 = jnp.bfloat16

POOL_WINDOWS = (2, 4, 8, 16)
POOL_HALO = 16
HEAD_DIM = 64
HEAD_V = 128
ROPE_THETA = 10000.0
RMS_EPS = 1e-6
SUBLN_EPS = 1e-5
N_MOD = 9
MASK_VALUE = -1e30
Q_SCALE = HEAD_DIM ** -0.5 * math.log2(math.e)
ONES_ROWS = 16
VT_ROWS = HEAD_V + ONES_ROWS

VMEM_LIMIT_BYTES = 56 * 1024 * 1024

TOKEN_TILE = 512
FF_CHUNK = 512
FFN_SLOTS = 3
FFN_NORM_PARTS = 8
Q_TILE = 512
KV_CHUNK = 1024
VT_CHUNK = 512
ATTN_SUB = 256


def _params(sem):
    return pltpu.CompilerParams(dimension_semantics=sem, vmem_limit_bytes=VMEM_LIMIT_BYTES)


def _resident(block_shape, index_map):
    return pl.BlockSpec(block_shape, index_map, pipeline_mode=pl.Buffered(1))


def _norm_modulate(x, g, shift, scale):
    y = x * lax.rsqrt(jnp.mean(x * x, axis=-1, keepdims=True) + RMS_EPS) * g
    return y * (1.0 + scale) + shift


def _mod_kernel(ct_ref, w_ref, b_ref, o_ref):
    ct = ct_ref[...]
    ct = ct * jax.nn.sigmoid(ct)
    w = w_ref[...]
    rows = [jnp.sum(w * ct[:, b:b + 1], axis=0, keepdims=True) for b in range(ct.shape[1])]
    o_ref[...] = jnp.concatenate(rows, axis=0) + b_ref[...]


def _modulation(c, w_mod, b_mod, tn=1024):
    L, D, N = w_mod.shape
    B = c.shape[0]
    return pl.pallas_call(
        _mod_kernel,
        grid=(L, N // tn),
        in_specs=[
            pl.BlockSpec((D, B), lambda l, n: (0, 0)),
            pl.BlockSpec((None, D, tn), lambda l, n: (l, 0, n)),
            pl.BlockSpec((None, 1, tn), lambda l, n: (l, 0, n)),
        ],
        out_specs=pl.BlockSpec((None, B, tn), lambda l, n: (l, 0, n)),
        out_shape=jax.ShapeDtypeStruct((L, B, N), F32),
        compiler_params=_params(("parallel", "parallel")),
        name="modulation",
    )(c.T, w_mod, b_mod.reshape(L, 1, N))


def _ffn_kernel(x_ref, xn_ref, mod_ref, modn_ref, g_ref, fg_ref, wgu_hbm, wo_hbm, o_ref,
                h_ref, hn_ref, gu_buf, wo_buf, sem, *, row0, final):
    i = pl.program_id(0)
    n_tiles = pl.num_programs(0)
    nf = wgu_hbm.shape[0]

    def chunk_copies(j, slot):
        return (
            pltpu.make_async_copy(wgu_hbm.at[j], gu_buf.at[slot], sem.at[slot, 0]),
            pltpu.make_async_copy(wo_hbm.at[j], wo_buf.at[slot], sem.at[slot, 1]),
        )

    def normed(xs, ms):
        return _norm_modulate(xs, g_ref[...], ms[row0:row0 + 1, :], ms[row0 + 1:row0 + 2, :]).astype(BF16)

    @pl.when(i == 0)
    def _():
        for ahead in range(FFN_SLOTS - 1):
            for cp in chunk_copies(ahead, ahead):
                cp.start()
        h_ref[...] = normed(x_ref[...], mod_ref)

    @pl.when(i > 0)
    def _():
        h_ref[...] = hn_ref[...]

    o_ref[...] = jnp.zeros_like(o_ref)
    parts = FFN_NORM_PARTS if nf >= FFN_NORM_PARTS else 1
    rows = x_ref.shape[0] // parts

    def body(j, carry):
        gstep = i * nf + j
        slot = gstep % FFN_SLOTS
        ahead = FFN_SLOTS - 1

        @pl.when(gstep + ahead < n_tiles * nf)
        def _():
            for cp in chunk_copies((j + ahead) % nf, (gstep + ahead) % FFN_SLOTS):
                cp.start()

        for cp in chunk_copies(j, slot):
            cp.wait()
        r0 = pl.multiple_of(jnp.minimum(j, parts - 1) * rows, rows)
        hn_ref[pl.ds(r0, rows), :] = normed(xn_ref[pl.ds(r0, rows), :], modn_ref)
        hh = h_ref[...]
        a_g = jnp.dot(hh, gu_buf[slot, 0], preferred_element_type=F32)
        a_u = jnp.dot(hh, gu_buf[slot, 1], preferred_element_type=F32)
        act = (a_g * jax.nn.sigmoid(a_g) * a_u).astype(BF16)
        o_ref[...] += jnp.dot(act, wo_buf[slot], preferred_element_type=F32)
        return carry

    lax.fori_loop(0, nf, body, 0)
    y = x_ref[...] + 0.5 * mod_ref[row0 + 2:row0 + 3, :] * o_ref[...]
    if final:
        y = y * lax.rsqrt(jnp.mean(y * y, axis=-1, keepdims=True) + RMS_EPS) * fg_ref[...]
    o_ref[...] = y


def _ffn_weights(w_in, w_out, tf):
    D = w_in.shape[0]
    F = w_out.shape[0]
    nf = F // tf
    w_gu = w_in.reshape(D, 2, nf, tf).transpose(2, 1, 0, 3).astype(BF16)
    return w_gu, w_out.reshape(nf, tf, D).astype(BF16)


def _ffn(x, mod, g, final_g, w_gu, w_o, *, row0, seq, tm, final=False):
    T, D = x.shape
    nf, tf, _ = w_o.shape
    tpb = seq // tm
    n_tiles = T // tm
    return pl.pallas_call(
        functools.partial(_ffn_kernel, row0=row0, final=final),
        grid=(n_tiles,),
        in_specs=[
            pl.BlockSpec((tm, D), lambda i: (i, 0)),
            pl.BlockSpec((tm, D), lambda i: (jnp.minimum(i + 1, n_tiles - 1), 0)),
            pl.BlockSpec((None, N_MOD, D), lambda i: (i // tpb, 0, 0)),
            pl.BlockSpec((None, N_MOD, D), lambda i: (jnp.minimum(i + 1, n_tiles - 1) // tpb, 0, 0)),
            pl.BlockSpec((1, D), lambda i: (0, 0)),
            pl.BlockSpec((1, D), lambda i: (0, 0)),
            pl.BlockSpec(memory_space=pl.ANY),
            pl.BlockSpec(memory_space=pl.ANY),
        ],
        out_specs=pl.BlockSpec((tm, D), lambda i: (i, 0)),
        out_shape=jax.ShapeDtypeStruct((T, D), F32),
        scratch_shapes=[
            pltpu.VMEM((tm, D), BF16),
            pltpu.VMEM((tm, D), BF16),
            pltpu.VMEM((FFN_SLOTS, 2, D, tf), BF16),
            pltpu.VMEM((FFN_SLOTS, tf, D), BF16),
            pltpu.SemaphoreType.DMA((FFN_SLOTS, 2)),
        ],
        compiler_params=_params(("arbitrary",)),
        name="ffn",
    )(x, x, mod, mod, g, final_g, w_gu, w_o)


def _mix_in_kernel(x_ref, mod_ref, g_ref, w_ref, wvt_ref, cos_ref, sin_ref, pw_ref, pb_ref, ps_ref,
                   q_ref, k_ref, vt_ref, yp_ref, pext_ref, *, tpb, width, n_heads):
    i = pl.program_id(0)
    tm = x_ref.shape[0]
    h = _norm_modulate(x_ref[...], g_ref[...], mod_ref[3:4, :], mod_ref[4:5, :]).astype(BF16)

    p = jnp.dot(h, w_ref[:, 0:width], preferred_element_type=F32)

    @pl.when(i % tpb == 0)
    def _():
        pext_ref[0:POOL_HALO, :] = jnp.zeros((POOL_HALO, width), F32)

    pext_ref[POOL_HALO:POOL_HALO + tm, :] = p
    pos = (i % tpb) * tm + lax.broadcasted_iota(jnp.int32, (tm, 1), 0)
    cg = width // len(POOL_WINDOWS)
    for gi, win in enumerate(POOL_WINDOWS):
        lo, hi = gi * cg, (gi + 1) * cg
        pg = p[:, lo:hi]
        acc = pg
        for back in range(1, win):
            acc = acc + pext_ref[POOL_HALO - back:POOL_HALO - back + tm, lo:hi]
        count = jnp.minimum(pos + 1, win).astype(F32)
        pooled = acc / count - pg
        mixed = jnp.dot(pooled.astype(BF16), pw_ref[gi], preferred_element_type=F32) + pb_ref[:, lo:hi]
        yp_ref[:, lo:hi] = (mixed * ps_ref[:, lo:hi]).astype(BF16)
    pext_ref[0:POOL_HALO, :] = pext_ref[tm:tm + POOL_HALO, :]

    cos = cos_ref[...]
    sin = sin_ref[...]
    lane = lax.broadcasted_iota(jnp.int32, (tm, HEAD_V), 1)
    first_half = (lane % HEAD_DIM) < (HEAD_DIM // 2)
    for sec, out_ref, mul in ((1, q_ref, Q_SCALE), (2, k_ref, None)):
        z = jnp.dot(h, w_ref[:, sec * width:(sec + 1) * width], preferred_element_type=F32)
        for hh in range(n_heads):
            zh = z[:, hh * HEAD_V:(hh + 1) * HEAD_V]
            partner = jnp.where(first_half,
                                pltpu.roll(zh, HEAD_V - HEAD_DIM // 2, axis=1),
                                pltpu.roll(zh, HEAD_DIM // 2, axis=1))
            r = zh * cos + partner * sin
            if mul is not None:
                r = r * mul
            out_ref[hh] = r.astype(BF16)

    tk = vt_ref.shape[3]
    zt = lax.dot_general(wvt_ref[...], h, (((1,), (1,)), ((), ())), preferred_element_type=F32)
    ones_rows = (lax.broadcasted_iota(jnp.int32, (ONES_ROWS, tk), 0) == 0).astype(BF16)
    for hh in range(n_heads):
        for ck in range(tm // tk):
            vt_ref[hh, ck, 0:HEAD_V, :] = zt[hh * HEAD_V:(hh + 1) * HEAD_V, ck * tk:(ck + 1) * tk].astype(BF16)
            vt_ref[hh, ck, HEAD_V:VT_ROWS, :] = ones_rows


def _mix_in(x, mod, g, w_in, w_vt, cos, sin, pool_w, pool_b, pool_scale, *, seq, tm, tk):
    T, D = x.shape
    width = w_vt.shape[0]
    n_heads = width // HEAD_V
    tpb = seq // tm
    head_major = jax.ShapeDtypeStruct((n_heads, T, HEAD_V), BF16)
    head_spec = pl.BlockSpec((n_heads, tm, HEAD_V), lambda i: (0, i, 0))
    vt_shape = jax.ShapeDtypeStruct((n_heads, T // tk, VT_ROWS, tk), BF16)
    vt_spec = pl.BlockSpec((n_heads, tm // tk, VT_ROWS, tk), lambda i: (0, i, 0, 0))
    return pl.pallas_call(
        functools.partial(_mix_in_kernel, tpb=tpb, width=width, n_heads=n_heads),
        grid=(T // tm,),
        in_specs=[
            pl.BlockSpec((tm, D), lambda i: (i, 0)),
            pl.BlockSpec((None, N_MOD, D), lambda i: (i // tpb, 0, 0)),
            pl.BlockSpec((1, D), lambda i: (0, 0)),
            _resident((D, 3 * width), lambda i: (0, 0)),
            _resident((width, D), lambda i: (0, 0)),
            pl.BlockSpec((tm, HEAD_V), lambda i: (i % tpb, 0)),
            pl.BlockSpec((tm, HEAD_V), lambda i: (i % tpb, 0)),
            _resident(pool_w.shape, lambda i: (0, 0, 0)),
            pl.BlockSpec((1, width), lambda i: (0, 0)),
            pl.BlockSpec((1, width), lambda i: (0, 0)),
        ],
        out_specs=[head_spec, head_spec, vt_spec, pl.BlockSpec((tm, width), lambda i: (i, 0))],
        out_shape=[head_major, head_major, vt_shape, jax.ShapeDtypeStruct((T, width), BF16)],
        scratch_shapes=[pltpu.VMEM((POOL_HALO + tm, width), F32)],
        compiler_params=_params(("arbitrary",)),
        name="mix_in",
    )(x, mod, g, w_in, w_vt, cos, sin, pool_w, pool_b, pool_scale)


def _attn_kernel(linit_ref, q_ref, k_ref, vt_ref, dl_ref, g_ref, o_ref,
                 qs_ref, sa_ref, sb_ref, mca_ref, mcb_ref, m_ref, acc_ref, *, tk):
    i = pl.program_id(2)
    tq = q_ref.shape[0]
    vc = vt_ref.shape[2]

    q = q_ref[...]
    lane = lax.broadcasted_iota(jnp.int32, q.shape, 1)
    qs_ref[0:tq, :] = jnp.where(lane < HEAD_DIM, q, jnp.zeros_like(q))
    qs_ref[tq:2 * tq, :] = jnp.where(lane < HEAD_DIM, jnp.zeros_like(q), q)
    m_ref[...] = jnp.full(m_ref.shape, MASK_VALUE, F32)
    acc_ref[...] = jnp.zeros(acc_ref.shape, F32)

    last = (i * tq) // tk

    buf_a = (sa_ref, mca_ref)
    buf_b = (sb_ref, mcb_ref)

    sub = min(ATTN_SUB, vc)
    n_sub = tk // sub

    def score_sub(j, u, masked):
        start = pl.multiple_of(j * tk + u * sub, sub)
        s = lax.dot_general(k_ref[pl.ds(start, sub), :], qs_ref[...], (((1,), (1,)), ((), ())),
                            preferred_element_type=F32)
        if masked:
            key_pos = start + lax.broadcasted_iota(jnp.int32, s.shape, 0)
            col = lax.broadcasted_iota(jnp.int32, s.shape, 1)
            q_pos = i * tq + jnp.where(col >= tq, col - tq, col)
            s = jnp.where(key_pos <= q_pos, s, MASK_VALUE)
        return s

    def step(j, cur, nxt, next_masked):
        if cur is not None:
            s_cur, mc_cur = cur
            m_prev = m_ref[...]
            m_new = jnp.maximum(m_prev, mc_cur[...])
            alpha = jnp.exp2(m_prev - m_new)
        pv = None
        mx = None
        for u in range(n_sub):
            if nxt is not None:
                s = score_sub(j + 1 if cur is not None else j, u, next_masked)
                nxt[0][u * sub:(u + 1) * sub, :] = s
                smax = jnp.max(s, axis=0, keepdims=True)
                mx = smax if mx is None else jnp.maximum(mx, smax)
            if cur is not None:
                p = jnp.exp2(s_cur[u * sub:(u + 1) * sub, :] - m_new).astype(BF16)
                lo = (u * sub) % vc
                vt = vt_ref[j * (tk // vc) + (u * sub) // vc, :, lo:lo + sub]
                d = jnp.dot(vt, p, preferred_element_type=F32)
                pv = d if pv is None else pv + d
        if nxt is not None:
            nxt[1][...] = mx
        if cur is not None:
            acc_ref[...] = alpha * acc_ref[...] + pv
            m_ref[...] = m_new

    @pl.when(last == 0)
    def _():
        step(0, None, buf_a, True)

    @pl.when(last > 0)
    def _():
        step(0, None, buf_a, False)

    trips = jnp.maximum(last - 1, 0) // 2

    def body(t, carry):
        step(2 * t, buf_a, buf_b, False)
        step(2 * t + 1, buf_b, buf_a, False)
        return carry

    lax.fori_loop(0, trips, body, 0)
    done = 2 * trips

    @pl.when(last == done)
    def _():
        step(last, buf_a, None, None)

    @pl.when(last == done + 1)
    def _():
        step(done, buf_a, buf_b, True)
        step(last, buf_b, None, None)

    @pl.when(last == done + 2)
    def _():
        step(done, buf_a, buf_b, False)
        step(done + 1, buf_b, buf_a, True)
        step(last, buf_a, None, None)

    dl = dl_ref[...]
    lambda_init = linit_ref[0]
    lam = (jnp.exp(jnp.sum(dl[0:1] * dl[1:2], axis=1, keepdims=True))
           - jnp.exp(jnp.sum(dl[2:3] * dl[3:4], axis=1, keepdims=True)) + lambda_init)
    num = acc_ref[0:HEAD_V, :]
    den = acc_ref[HEAD_V:HEAD_V + 1, :]
    inv = 1.0 / den
    o = num[:, 0:tq] * inv[:, 0:tq] - lam * (num[:, tq:2 * tq] * inv[:, tq:2 * tq])
    o = o * lax.rsqrt(jnp.mean(o * o, axis=0, keepdims=True) + SUBLN_EPS)
    o_ref[...] = (o.T * g_ref[...] * (1.0 - lambda_init)).astype(BF16)


def _attention(linit, q, k, vt, diff_lambda, subln, *, batch, seq, tq, tk):
    n_heads, T, _ = q.shape
    vc = vt.shape[3]
    nq = seq // tq
    nk = seq // vc
    return pl.pallas_call(
        functools.partial(_attn_kernel, tk=tk),
        grid=(batch, n_heads, nq),
        in_specs=[
            pl.BlockSpec(memory_space=pltpu.SMEM),
            pl.BlockSpec((None, tq, HEAD_V), lambda b, h, i: (h, b * nq + i, 0)),
            pl.BlockSpec((None, seq, HEAD_V), lambda b, h, i: (h, b, 0)),
            pl.BlockSpec((None, nk, VT_ROWS, vc), lambda b, h, i: (h, b, 0, 0)),
            pl.BlockSpec(diff_lambda.shape, lambda b, h, i: (0, 0)),
            pl.BlockSpec((1, HEAD_V), lambda b, h, i: (0, 0)),
        ],
        out_specs=pl.BlockSpec((tq, HEAD_V), lambda b, h, i: (b * nq + i, h)),
        out_shape=jax.ShapeDtypeStruct((T, n_heads * HEAD_V), BF16),
        scratch_shapes=[
            pltpu.VMEM((2 * tq, HEAD_V), BF16),
            pltpu.VMEM((tk, 2 * tq), F32),
            pltpu.VMEM((tk, 2 * tq), F32),
            pltpu.VMEM((1, 2 * tq), F32),
            pltpu.VMEM((1, 2 * tq), F32),
            pltpu.VMEM((1, 2 * tq), F32),
            pltpu.VMEM((VT_ROWS, 2 * tq), F32),
        ],
        compiler_params=_params(("parallel", "parallel", "arbitrary")),
        name="diff_attn",
    )(linit, q, k, vt, diff_lambda, subln)


def _mix_out_kernel(x_ref, mod_ref, yp_ref, ya_ref, w_ref, o_ref):
    width = yp_ref.shape[1]
    y = jnp.dot(yp_ref[...], w_ref[0:width, :], preferred_element_type=F32)
    y = y + jnp.dot(ya_ref[...], w_ref[width:2 * width, :], preferred_element_type=F32)
    o_ref[...] = x_ref[...] + mod_ref[5:6, :] * y


def _mix_out(x, mod, y_pool, y_attn, w_out, *, seq, tm):
    T, D = x.shape
    width = y_pool.shape[1]
    tpb = seq // tm
    return pl.pallas_call(
        _mix_out_kernel,
        grid=(T // tm,),
        in_specs=[
            pl.BlockSpec((tm, D), lambda i: (i, 0)),
            pl.BlockSpec((None, N_MOD, D), lambda i: (i // tpb, 0, 0)),
            pl.BlockSpec((tm, width), lambda i: (i, 0)),
            pl.BlockSpec((tm, width), lambda i: (i, 0)),
            _resident(w_out.shape, lambda i: (0, 0)),
        ],
        out_specs=pl.BlockSpec((tm, D), lambda i: (i, 0)),
        out_shape=jax.ShapeDtypeStruct((T, D), F32),
        compiler_params=_params(("parallel",)),
        name="mix_out",
    )(x, mod, y_pool, y_attn, w_out)


def _rope_tables(seq):
    pos = jnp.arange(seq, dtype=F32)
    inv_freq = ROPE_THETA ** (-jnp.arange(0, HEAD_DIM, 2, dtype=F32) / HEAD_DIM)
    ang = pos[:, None] * inv_freq[None, :]
    c, s = jnp.cos(ang), jnp.sin(ang)
    return jnp.concatenate([c, c, c, c], axis=1), jnp.concatenate([-s, s, -s, s], axis=1)


def kernel(x, c, w_mod, b_mod, norm_ffn1, ffn1_w_in, ffn1_w_out, norm_mix, w_in, pool_w, pool_b, pool_scale,
           diff_lambda, diff_subln, w_out, norm_ffn2, ffn2_w_in, ffn2_w_out, final_norm):
    B, S, D = x.shape
    L = w_mod.shape[0]
    F = ffn1_w_out.shape[1]
    tm = min(TOKEN_TILE, S)
    tq = min(Q_TILE, S)
    tk = min(KV_CHUNK, S)
    vc = min(VT_CHUNK, tk, tm)
    tf = FF_CHUNK if F % FF_CHUNK == 0 else F
    assert S % tm == 0 and S % tk == 0 and tm % vc == 0 and tk % vc == 0 and tk % tq == 0
    width = pool_scale.shape[1]

    cos, sin = _rope_tables(S)
    final_g = final_norm.reshape(1, D)
    mod_all = _modulation(c, w_mod, b_mod).reshape(L, B, N_MOD, D)
    xt = x.reshape(B * S, D)
    for l in range(L):
        mod = mod_all[l]
        linit = jnp.full((1,), 0.8 - 0.6 * math.exp(-0.3 * l), F32)
        xt = _ffn(xt, mod, norm_ffn1[l].reshape(1, D), final_g, *_ffn_weights(ffn1_w_in[l], ffn1_w_out[l], tf),
                  row0=0, seq=S, tm=tm)
        w_pqk = w_in[l, :, :3 * width].astype(BF16)
        w_vt = w_in[l, :, 3 * width:].T.astype(BF16)
        q, k, vt, y_pool = _mix_in(xt, mod, norm_mix[l].reshape(1, D), w_pqk, w_vt, cos, sin,
                                   pool_w[l].astype(BF16), pool_b[l].reshape(1, width),
                                   pool_scale[l].reshape(1, width), seq=S, tm=tm, tk=vc)
        y_attn = _attention(linit, q, k, vt, diff_lambda[l], diff_subln[l].reshape(1, HEAD_V),
                            batch=B, seq=S, tq=tq, tk=tk)
        xt = _mix_out(xt, mod, y_pool, y_attn, w_out[l].astype(BF16), seq=S, tm=tm)
        xt = _ffn(xt, mod, norm_ffn2[l].reshape(1, D), final_g, *_ffn_weights(ffn2_w_in[l], ffn2_w_out[l], tf),
                  row0=6, seq=S, tm=tm, final=(l == L - 1))
    return xt.reshape(B, S, D)
```

```python
import functools
import math

import jax
import jax.numpy as jnp
from jax import lax
from jax.experimental import pallas as pl
from jax.experimental.pallas import tpu as pltpu

F32 = jnp.float32
BF16 = jnp.bfloat16

POOL_WINDOWS = (2, 4, 8, 16)
POOL_HALO = 16
HEAD_DIM = 64
HEAD_V = 128
ROPE_THETA = 10000.0
RMS_EPS = 1e-6
SUBLN_EPS = 1e-5
N_MOD = 9
MASK_VALUE = -1e30
Q_SCALE = HEAD_DIM ** -0.5 * math.log2(math.e)
ONES_ROWS = 16
VT_ROWS = HEAD_V + ONES_ROWS

VMEM_LIMIT_BYTES = 56 * 1024 * 1024

TOKEN_TILE = 512
FF_CHUNK = 512
FFN_SLOTS = 3
FFN_NORM_PARTS = 8
Q_TILE = 512
KV_CHUNK = 1024
VT_CHUNK = 512
ATTN_SUB = 256


def _params(sem):
    return pltpu.CompilerParams(dimension_semantics=sem, vmem_limit_bytes=VMEM_LIMIT_BYTES)


def _resident(block_shape, index_map):
    return pl.BlockSpec(block_shape, index_map, pipeline_mode=pl.Buffered(1))


def _norm_modulate(x, g, shift, scale):
    y = x * lax.rsqrt(jnp.mean(x * x, axis=-1, keepdims=True) + RMS_EPS) * g
    return y * (1.0 + scale) + shift


def _mod_kernel(ct_ref, w_ref, b_ref, o_ref):
    ct = ct_ref[...]
    ct = ct * jax.nn.sigmoid(ct)
    w = w_ref[...]
    rows = [jnp.sum(w * ct[:, b:b + 1], axis=0, keepdims=True) for b in range(ct.shape[1])]
    o_ref[...] = jnp.concatenate(rows, axis=0) + b_ref[...]


def _modulation(c, w_mod, b_mod, tn=1024):
    L, D, N = w_mod.shape
    B = c.shape[0]
    return pl.pallas_call(
        _mod_kernel,
        grid=(L, N // tn),
        in_specs=[
            pl.BlockSpec((D, B), lambda l, n: (0, 0)),
            pl.BlockSpec((None, D, tn), lambda l, n: (l, 0, n)),
            pl.BlockSpec((None, 1, tn), lambda l, n: (l, 0, n)),
        ],
        out_specs=pl.BlockSpec((None, B, tn), lambda l, n: (l, 0, n)),
        out_shape=jax.ShapeDtypeStruct((L, B, N), F32),
        compiler_params=_params(("parallel", "parallel")),
        name="modulation",
    )(c.T, w_mod, b_mod.reshape(L, 1, N))


def _ffn_kernel(x_ref, xn_ref, mod_ref, modn_ref, g_ref, fg_ref, win_hbm, wo_hbm, o_ref,
                h_ref, hn_ref, gu_buf, wo_buf, sem, *, row0, final):
    i = pl.program_id(0)
    n_tiles = pl.num_programs(0)
    tf = wo_buf.shape[1]
    nf = wo_hbm.shape[0] // tf

    def chunk_copies(j, slot):
        col = j * tf if isinstance(j, int) else pl.multiple_of(j * tf, tf)
        return (
            pltpu.make_async_copy(win_hbm.at[:, pl.ds(col, tf)], gu_buf.at[slot, 0], sem.at[slot, 0]),
            pltpu.make_async_copy(win_hbm.at[:, pl.ds(nf * tf + col, tf)], gu_buf.at[slot, 1], sem.at[slot, 1]),
            pltpu.make_async_copy(wo_hbm.at[pl.ds(col, tf), :], wo_buf.at[slot], sem.at[slot, 2]),
        )

    def normed(xs, ms):
        return _norm_modulate(xs, g_ref[...], ms[row0:row0 + 1, :], ms[row0 + 1:row0 + 2, :]).astype(BF16)

    @pl.when(i == 0)
    def _():
        for ahead in range(FFN_SLOTS - 1):
            for cp in chunk_copies(ahead, ahead):
                cp.start()
        h_ref[...] = normed(x_ref[...], mod_ref)

    @pl.when(i > 0)
    def _():
        h_ref[...] = hn_ref[...]

    o_ref[...] = jnp.zeros_like(o_ref)
    parts = FFN_NORM_PARTS if nf >= FFN_NORM_PARTS else 1
    rows = x_ref.shape[0] // parts

    def body(j, carry):
        gstep = i * nf + j
        slot = gstep % FFN_SLOTS
        ahead = FFN_SLOTS - 1

        @pl.when(gstep + ahead < n_tiles * nf)
        def _():
            for cp in chunk_copies((j + ahead) % nf, (gstep + ahead) % FFN_SLOTS):
                cp.start()

        for cp in chunk_copies(j, slot):
            cp.wait()
        r0 = pl.multiple_of(jnp.minimum(j, parts - 1) * rows, rows)
        hn_ref[pl.ds(r0, rows), :] = normed(xn_ref[pl.ds(r0, rows), :], modn_ref)
        hh = h_ref[...]
        a_g = jnp.dot(hh, gu_buf[slot, 0], preferred_element_type=F32)
        a_u = jnp.dot(hh, gu_buf[slot, 1], preferred_element_type=F32)
        act = (a_g * jax.nn.sigmoid(a_g) * a_u).astype(BF16)
        o_ref[...] += jnp.dot(act, wo_buf[slot], preferred_element_type=F32)
        return carry

    lax.fori_loop(0, nf, body, 0)
    y = x_ref[...] + 0.5 * mod_ref[row0 + 2:row0 + 3, :] * o_ref[...]
    if final:
        y = y * lax.rsqrt(jnp.mean(y * y, axis=-1, keepdims=True) + RMS_EPS) * fg_ref[...]
    o_ref[...] = y


def _ffn(x, mod, g, final_g, w_in, w_o, *, row0, seq, tm, tf, final=False):
    T, D = x.shape
    tpb = seq // tm
    n_tiles = T // tm
    return pl.pallas_call(
        functools.partial(_ffn_kernel, row0=row0, final=final),
        grid=(n_tiles,),
        in_specs=[
            pl.BlockSpec((tm, D), lambda i: (i, 0)),
            pl.BlockSpec((tm, D), lambda i: (jnp.minimum(i + 1, n_tiles - 1), 0)),
            pl.BlockSpec((None, N_MOD, D), lambda i: (i // tpb, 0, 0)),
            pl.BlockSpec((None, N_MOD, D), lambda i: (jnp.minimum(i + 1, n_tiles - 1) // tpb, 0, 0)),
            pl.BlockSpec((1, D), lambda i: (0, 0)),
            pl.BlockSpec((1, D), lambda i: (0, 0)),
            pl.BlockSpec(memory_space=pl.ANY),
            pl.BlockSpec(memory_space=pl.ANY),
        ],
        out_specs=pl.BlockSpec((tm, D), lambda i: (i, 0)),
        out_shape=jax.ShapeDtypeStruct((T, D), F32),
        scratch_shapes=[
            pltpu.VMEM((tm, D), BF16),
            pltpu.VMEM((tm, D), BF16),
            pltpu.VMEM((FFN_SLOTS, 2, D, tf), BF16),
            pltpu.VMEM((FFN_SLOTS, tf, D), BF16),
            pltpu.SemaphoreType.DMA((FFN_SLOTS, 3)),
        ],
        compiler_params=_params(("arbitrary",)),
        name="ffn",
    )(x, x, mod, mod, g, final_g, w_in, w_o)


def _mix_in_kernel(x_ref, mod_ref, g_ref, w_ref, wvt_ref, cos_ref, sin_ref, pw_ref, pb_ref, ps_ref,
                   q_ref, k_ref, vt_ref, yp_ref, pext_ref, *, tpb, width, n_heads):
    i = pl.program_id(0)
    tm = x_ref.shape[0]
    h = _norm_modulate(x_ref[...], g_ref[...], mod_ref[3:4, :], mod_ref[4:5, :]).astype(BF16)

    p = jnp.dot(h, w_ref[:, 0:width], preferred_element_type=F32)

    @pl.when(i % tpb == 0)
    def _():
        pext_ref[0:POOL_HALO, :] = jnp.zeros((POOL_HALO, width), F32)

    pext_ref[POOL_HALO:POOL_HALO + tm, :] = p
    pos = (i % tpb) * tm + lax.broadcasted_iota(jnp.int32, (tm, 1), 0)
    cg = width // len(POOL_WINDOWS)
    for gi, win in enumerate(POOL_WINDOWS):
        lo, hi = gi * cg, (gi + 1) * cg
        pg = p[:, lo:hi]
        acc = pg
        for back in range(1, win):
            acc = acc + pext_ref[POOL_HALO - back:POOL_HALO - back + tm, lo:hi]
        count = jnp.minimum(pos + 1, win).astype(F32)
        pooled = acc / count - pg
        mixed = jnp.dot(pooled.astype(BF16), pw_ref[gi], preferred_element_type=F32) + pb_ref[:, lo:hi]
        yp_ref[:, lo:hi] = (mixed * ps_ref[:, lo:hi]).astype(BF16)
    pext_ref[0:POOL_HALO, :] = pext_ref[tm:tm + POOL_HALO, :]

    cos = cos_ref[...]
    sin = sin_ref[...]
    lane = lax.broadcasted_iota(jnp.int32, (tm, HEAD_V), 1)
    first_half = (lane % HEAD_DIM) < (HEAD_DIM // 2)
    for sec, out_ref, mul in ((1, q_ref, Q_SCALE), (2, k_ref, None)):
        z = jnp.dot(h, w_ref[:, sec * width:(sec + 1) * width], preferred_element_type=F32)
        for hh in range(n_heads):
            zh = z[:, hh * HEAD_V:(hh + 1) * HEAD_V]
            partner = jnp.where(first_half,
                                pltpu.roll(zh, HEAD_V - HEAD_DIM // 2, axis=1),
                                pltpu.roll(zh, HEAD_DIM // 2, axis=1))
            r = zh * cos + partner * sin
            if mul is not None:
                r = r * mul
            out_ref[hh] = r.astype(BF16)

    tk = vt_ref.shape[3]
    zt = lax.dot_general(wvt_ref[...], h, (((1,), (1,)), ((), ())), preferred_element_type=F32)
    ones_rows = (lax.broadcasted_iota(jnp.int32, (ONES_ROWS, tk), 0) == 0).astype(BF16)
    for hh in range(n_heads):
        for ck in range(tm // tk):
            vt_ref[hh, ck, 0:HEAD_V, :] = zt[hh * HEAD_V:(hh + 1) * HEAD_V, ck * tk:(ck + 1) * tk].astype(BF16)
            vt_ref[hh, ck, HEAD_V:VT_ROWS, :] = ones_rows


def _mix_in(x, mod, g, w_in, w_vt, cos, sin, pool_w, pool_b, pool_scale, *, seq, tm, tk):
    T, D = x.shape
    width = w_vt.shape[0]
    n_heads = width // HEAD_V
    tpb = seq // tm
    head_major = jax.ShapeDtypeStruct((n_heads, T, HEAD_V), BF16)
    head_spec = pl.BlockSpec((n_heads, tm, HEAD_V), lambda i: (0, i, 0))
    vt_shape = jax.ShapeDtypeStruct((n_heads, T // tk, VT_ROWS, tk), BF16)
    vt_spec = pl.BlockSpec((n_heads, tm // tk, VT_ROWS, tk), lambda i: (0, i, 0, 0))
    return pl.pallas_call(
        functools.partial(_mix_in_kernel, tpb=tpb, width=width, n_heads=n_heads),
        grid=(T // tm,),
        in_specs=[
            pl.BlockSpec((tm, D), lambda i: (i, 0)),
            pl.BlockSpec((None, N_MOD, D), lambda i: (i // tpb, 0, 0)),
            pl.BlockSpec((1, D), lambda i: (0, 0)),
            _resident((D, 3 * width), lambda i: (0, 0)),
            _resident((width, D), lambda i: (0, 0)),
            pl.BlockSpec((tm, HEAD_V), lambda i: (i % tpb, 0)),
            pl.BlockSpec((tm, HEAD_V), lambda i: (i % tpb, 0)),
            _resident(pool_w.shape, lambda i: (0, 0, 0)),
            pl.BlockSpec((1, width), lambda i: (0, 0)),
            pl.BlockSpec((1, width), lambda i: (0, 0)),
        ],
        out_specs=[head_spec, head_spec, vt_spec, pl.BlockSpec((tm, width), lambda i: (i, 0))],
        out_shape=[head_major, head_major, vt_shape, jax.ShapeDtypeStruct((T, width), BF16)],
        scratch_shapes=[pltpu.VMEM((POOL_HALO + tm, width), F32)],
        compiler_params=_params(("arbitrary",)),
        name="mix_in",
    )(x, mod, g, w_in, w_vt, cos, sin, pool_w, pool_b, pool_scale)


def _attn_kernel(linit_ref, q_ref, k_ref, vt_ref, dl_ref, g_ref, o_ref,
                 qs_ref, sa_ref, sb_ref, mca_ref, mcb_ref, m_ref, acc_ref, *, tk):
    i = pl.program_id(2)
    tq = q_ref.shape[0]
    vc = vt_ref.shape[2]

    q = q_ref[...]
    lane = lax.broadcasted_iota(jnp.int32, q.shape, 1)
    qs_ref[0:tq, :] = jnp.where(lane < HEAD_DIM, q, jnp.zeros_like(q))
    qs_ref[tq:2 * tq, :] = jnp.where(lane < HEAD_DIM, jnp.zeros_like(q), q)
    m_ref[...] = jnp.full(m_ref.shape, MASK_VALUE, F32)
    acc_ref[...] = jnp.zeros(acc_ref.shape, F32)

    last = (i * tq) // tk

    buf_a = (sa_ref, mca_ref)
    buf_b = (sb_ref, mcb_ref)

    sub = min(ATTN_SUB, vc)
    n_sub = tk // sub

    def score_sub(j, u, masked):
        start = pl.multiple_of(j * tk + u * sub, sub)
        s = lax.dot_general(k_ref[pl.ds(start, sub), :], qs_ref[...], (((1,), (1,)), ((), ())),
                            preferred_element_type=F32)
        if masked:
            key_pos = start + lax.broadcasted_iota(jnp.int32, s.shape, 0)
            col = lax.broadcasted_iota(jnp.int32, s.shape, 1)
            q_pos = i * tq + jnp.where(col >= tq, col - tq, col)
            s = jnp.where(key_pos <= q_pos, s, MASK_VALUE)
        return s

    def step(j, cur, nxt, next_masked):
        if cur is not None:
            s_cur, mc_cur = cur
            m_prev = m_ref[...]
            m_new = jnp.maximum(m_prev, mc_cur[...])
            alpha = jnp.exp2(m_prev - m_new)
        pv = None
        mx = None
        for u in range(n_sub):
            if nxt is not None:
                s = score_sub(j + 1 if cur is not None else j, u, next_masked)
                nxt[0][u * sub:(u + 1) * sub, :] = s
                smax = jnp.max(s, axis=0, keepdims=True)
                mx = smax if mx is None else jnp.maximum(mx, smax)
            if cur is not None:
                p = jnp.exp2(s_cur[u * sub:(u + 1) * sub, :] - m_new).astype(BF16)
                lo = (u * sub) % vc
                vt = vt_ref[j * (tk // vc) + (u * sub) // vc, :, lo:lo + sub]
                d = jnp.dot(vt, p, preferred_element_type=F32)
                pv = d if pv is None else pv + d
        if nxt is not None:
            nxt[1][...] = mx
        if cur is not None:
            acc_ref[...] = alpha * acc_ref[...] + pv
            m_ref[...] = m_new

    @pl.when(last == 0)
    def _():
        step(0, None, buf_a, True)

    @pl.when(last > 0)
    def _():
        step(0, None, buf_a, False)

    trips = jnp.maximum(last - 1, 0) // 2

    def body(t, carry):
        step(2 * t, buf_a, buf_b, False)
        step(2 * t + 1, buf_b, buf_a, False)
        return carry

    lax.fori_loop(0, trips, body, 0)
    done = 2 * trips

    @pl.when(last == done)
    def _():
        step(last, buf_a, None, None)

    @pl.when(last == done + 1)
    def _():
        step(done, buf_a, buf_b, True)
        step(last, buf_b, None, None)

    @pl.when(last == done + 2)
    def _():
        step(done, buf_a, buf_b, False)
        step(done + 1, buf_b, buf_a, True)
        step(last, buf_a, None, None)

    dl = dl_ref[...]
    lambda_init = linit_ref[0]
    lam = (jnp.exp(jnp.sum(dl[0:1] * dl[1:2], axis=1, keepdims=True))
           - jnp.exp(jnp.sum(dl[2:3] * dl[3:4], axis=1, keepdims=True)) + lambda_init)
    num = acc_ref[0:HEAD_V, :]
    den = acc_ref[HEAD_V:HEAD_V + 1, :]
    inv = 1.0 / den
    o = num[:, 0:tq] * inv[:, 0:tq] - lam * (num[:, tq:2 * tq] * inv[:, tq:2 * tq])
    o = o * lax.rsqrt(jnp.mean(o * o, axis=0, keepdims=True) + SUBLN_EPS)
    o_ref[...] = (o.T * g_ref[...] * (1.0 - lambda_init)).astype(BF16)


def _attention(linit, q, k, vt, diff_lambda, subln, *, batch, seq, tq, tk):
    n_heads, T, _ = q.shape
    vc = vt.shape[3]
    nq = seq // tq
    nk = seq // vc
    return pl.pallas_call(
        functools.partial(_attn_kernel, tk=tk),
        grid=(batch, n_heads, nq),
        in_specs=[
            pl.BlockSpec(memory_space=pltpu.SMEM),
            pl.BlockSpec((None, tq, HEAD_V), lambda b, h, i: (h, b * nq + i, 0)),
            pl.BlockSpec((None, seq, HEAD_V), lambda b, h, i: (h, b, 0)),
            pl.BlockSpec((None, nk, VT_ROWS, vc), lambda b, h, i: (h, b, 0, 0)),
            pl.BlockSpec(diff_lambda.shape, lambda b, h, i: (0, 0)),
            pl.BlockSpec((1, HEAD_V), lambda b, h, i: (0, 0)),
        ],
        out_specs=pl.BlockSpec((tq, HEAD_V), lambda b, h, i: (b * nq + i, h)),
        out_shape=jax.ShapeDtypeStruct((T, n_heads * HEAD_V), BF16),
        scratch_shapes=[
            pltpu.VMEM((2 * tq, HEAD_V), BF16),
            pltpu.VMEM((tk, 2 * tq), F32),
            pltpu.VMEM((tk, 2 * tq), F32),
            pltpu.VMEM((1, 2 * tq), F32),
            pltpu.VMEM((1, 2 * tq), F32),
            pltpu.VMEM((1, 2 * tq), F32),
            pltpu.VMEM((VT_ROWS, 2 * tq), F32),
        ],
        compiler_params=_params(("parallel", "parallel", "arbitrary")),
        name="diff_attn",
    )(linit, q, k, vt, diff_lambda, subln)


def _mix_out_kernel(x_ref, mod_ref, yp_ref, ya_ref, w_ref, o_ref):
    width = yp_ref.shape[1]
    y = jnp.dot(yp_ref[...], w_ref[0:width, :], preferred_element_type=F32)
    y = y + jnp.dot(ya_ref[...], w_ref[width:2 * width, :], preferred_element_type=F32)
    o_ref[...] = x_ref[...] + mod_ref[5:6, :] * y


def _mix_out(x, mod, y_pool, y_attn, w_out, *, seq, tm):
    T, D = x.shape
    width = y_pool.shape[1]
    tpb = seq // tm
    return pl.pallas_call(
        _mix_out_kernel,
        grid=(T // tm,),
        in_specs=[
            pl.BlockSpec((tm, D), lambda i: (i, 0)),
            pl.BlockSpec((None, N_MOD, D), lambda i: (i // tpb, 0, 0)),
            pl.BlockSpec((tm, width), lambda i: (i, 0)),
            pl.BlockSpec((tm, width), lambda i: (i, 0)),
            _resident(w_out.shape, lambda i: (0, 0)),
        ],
        out_specs=pl.BlockSpec((tm, D), lambda i: (i, 0)),
        out_shape=jax.ShapeDtypeStruct((T, D), F32),
        compiler_params=_params(("parallel",)),
        name="mix_out",
    )(x, mod, y_pool, y_attn, w_out)


def _rope_tables(seq):
    pos = jnp.arange(seq, dtype=F32)
    inv_freq = ROPE_THETA ** (-jnp.arange(0, HEAD_DIM, 2, dtype=F32) / HEAD_DIM)
    ang = pos[:, None] * inv_freq[None, :]
    c, s = jnp.cos(ang), jnp.sin(ang)
    return jnp.concatenate([c, c, c, c], axis=1), jnp.concatenate([-s, s, -s, s], axis=1)


def kernel(x, c, w_mod, b_mod, norm_ffn1, ffn1_w_in, ffn1_w_out, norm_mix, w_in, pool_w, pool_b, pool_scale,
           diff_lambda, diff_subln, w_out, norm_ffn2, ffn2_w_in, ffn2_w_out, final_norm):
    B, S, D = x.shape
    L = w_mod.shape[0]
    F = ffn1_w_out.shape[1]
    tm = min(TOKEN_TILE, S)
    tq = min(Q_TILE, S)
    tk = min(KV_CHUNK, S)
    vc = min(VT_CHUNK, tk, tm)
    tf = FF_CHUNK if F % FF_CHUNK == 0 else F
    assert S % tm == 0 and S % tk == 0 and tm % vc == 0 and tk % vc == 0 and tk % tq == 0
    width = pool_scale.shape[1]

    cos, sin = _rope_tables(S)
    final_g = final_norm.reshape(1, D)
    mod_all = _modulation(c, w_mod, b_mod).reshape(L, B, N_MOD, D)
    xt = x.reshape(B * S, D)
    for l in range(L):
        mod = mod_all[l]
        linit = jnp.full((1,), 0.8 - 0.6 * math.exp(-0.3 * l), F32)
        xt = _ffn(xt, mod, norm_ffn1[l].reshape(1, D), final_g, ffn1_w_in[l].astype(BF16),
                  ffn1_w_out[l].astype(BF16), row0=0, seq=S, tm=tm, tf=tf)
        w_pqk = w_in[l, :, :3 * width].astype(BF16)
        w_vt = w_in[l, :, 3 * width:].T.astype(BF16)
        q, k, vt, y_pool = _mix_in(xt, mod, norm_mix[l].reshape(1, D), w_pqk, w_vt, cos, sin,
                                   pool_w[l].astype(BF16), pool_b[l].reshape(1, width),
                                   pool_scale[l].reshape(1, width), seq=S, tm=tm, tk=vc)
        y_attn = _attention(linit, q, k, vt, diff_lambda[l], diff_subln[l].reshape(1, HEAD_V),
                            batch=B, seq=S, tq=tq, tk=tk)
        xt = _mix_out(xt, mod, y_pool, y_attn, w_out[l].astype(BF16), seq=S, tm=tm)
        xt = _ffn(xt, mod, norm_ffn2[l].reshape(1, D), final_g, ffn2_w_in[l].astype(BF16),
                  ffn2_w_out[l].astype(BF16), row0=6, seq=S, tm=tm, tf=tf, final=(l == L - 1))
    return xt.reshape(B, S, D)
```

```python
import functools
import math

import jax
import jax.numpy as jnp
from jax import lax
from jax.experimental import pallas as pl
from jax.experimental.pallas import tpu as pltpu

F32 = jnp.float32
BF16 = jnp.bfloat16

POOL_WINDOWS = (2, 4, 8, 16)
POOL_HALO = 16
HEAD_DIM = 64
HEAD_V = 128
ROPE_THETA = 10000.0
RMS_EPS = 1e-6
SUBLN_EPS = 1e-5
N_MOD = 9
MASK_VALUE = -1e30
Q_SCALE = HEAD_DIM ** -0.5 * math.log2(math.e)
ONES_ROWS = 16
VT_ROWS = HEAD_V + ONES_ROWS

VMEM_LIMIT_BYTES = 56 * 1024 * 1024

TOKEN_TILE = 512
FF_CHUNK = 512
FFN_SLOTS = 3
FFN_NORM_PARTS = 8
Q_TILE = 1024
KV_CHUNK = 1024
VT_CHUNK = 512
ATTN_SUB = 256


def _params(sem):
    return pltpu.CompilerParams(dimension_semantics=sem, vmem_limit_bytes=VMEM_LIMIT_BYTES)


def _resident(block_shape, index_map):
    return pl.BlockSpec(block_shape, index_map, pipeline_mode=pl.Buffered(1))


def _norm_modulate(x, g, shift, scale):
    y = x * lax.rsqrt(jnp.mean(x * x, axis=-1, keepdims=True) + RMS_EPS) * g
    return y * (1.0 + scale) + shift


def _mod_kernel(ct_ref, w_ref, b_ref, o_ref):
    ct = ct_ref[...]
    ct = ct * jax.nn.sigmoid(ct)
    w = w_ref[...]
    rows = [jnp.sum(w * ct[:, b:b + 1], axis=0, keepdims=True) for b in range(ct.shape[1])]
    o_ref[...] = jnp.concatenate(rows, axis=0) + b_ref[...]


def _modulation(c, w_mod, b_mod, tn=1024):
    L, D, N = w_mod.shape
    B = c.shape[0]
    return pl.pallas_call(
        _mod_kernel,
        grid=(L, N // tn),
        in_specs=[
            pl.BlockSpec((D, B), lambda l, n: (0, 0)),
            pl.BlockSpec((None, D, tn), lambda l, n: (l, 0, n)),
            pl.BlockSpec((None, 1, tn), lambda l, n: (l, 0, n)),
        ],
        out_specs=pl.BlockSpec((None, B, tn), lambda l, n: (l, 0, n)),
        out_shape=jax.ShapeDtypeStruct((L, B, N), F32),
        compiler_params=_params(("parallel", "parallel")),
        name="modulation",
    )(c.T, w_mod, b_mod.reshape(L, 1, N))


def _ffn_kernel(x_ref, xn_ref, mod_ref, modn_ref, g_ref, fg_ref, win_hbm, wo_hbm, o_ref,
                h_ref, hn_ref, gu_buf, wo_buf, sem, *, row0, final):
    i = pl.program_id(0)
    n_tiles = pl.num_programs(0)
    tf = wo_buf.shape[1]
    nf = wo_hbm.shape[0] // tf

    def chunk_copies(j, slot):
        col = j * tf if isinstance(j, int) else pl.multiple_of(j * tf, tf)
        return (
            pltpu.make_async_copy(win_hbm.at[:, pl.ds(col, tf)], gu_buf.at[slot, 0], sem.at[slot, 0]),
            pltpu.make_async_copy(win_hbm.at[:, pl.ds(nf * tf + col, tf)], gu_buf.at[slot, 1], sem.at[slot, 1]),
            pltpu.make_async_copy(wo_hbm.at[pl.ds(col, tf), :], wo_buf.at[slot], sem.at[slot, 2]),
        )

    def normed(xs, ms):
        return _norm_modulate(xs, g_ref[...], ms[row0:row0 + 1, :], ms[row0 + 1:row0 + 2, :]).astype(BF16)

    @pl.when(i == 0)
    def _():
        for ahead in range(FFN_SLOTS - 1):
            for cp in chunk_copies(ahead, ahead):
                cp.start()
        h_ref[...] = normed(x_ref[...], mod_ref)

    @pl.when(i > 0)
    def _():
        h_ref[...] = hn_ref[...]

    o_ref[...] = jnp.zeros_like(o_ref)
    parts = FFN_NORM_PARTS if nf >= FFN_NORM_PARTS else 1
    rows = x_ref.shape[0] // parts

    def body(j, carry):
        gstep = i * nf + j
        slot = gstep % FFN_SLOTS
        ahead = FFN_SLOTS - 1

        @pl.when(gstep + ahead < n_tiles * nf)
        def _():
            for cp in chunk_copies((j + ahead) % nf, (gstep + ahead) % FFN_SLOTS):
                cp.start()

        for cp in chunk_copies(j, slot):
            cp.wait()
        r0 = pl.multiple_of(jnp.minimum(j, parts - 1) * rows, rows)
        hn_ref[pl.ds(r0, rows), :] = normed(xn_ref[pl.ds(r0, rows), :], modn_ref)
        hh = h_ref[...]
        a_g = jnp.dot(hh, gu_buf[slot, 0], preferred_element_type=F32)
        a_u = jnp.dot(hh, gu_buf[slot, 1], preferred_element_type=F32)
        act = (a_g * jax.nn.sigmoid(a_g) * a_u).astype(BF16)
        o_ref[...] += jnp.dot(act, wo_buf[slot], preferred_element_type=F32)
        return carry

    lax.fori_loop(0, nf, body, 0)
    y = x_ref[...] + 0.5 * mod_ref[row0 + 2:row0 + 3, :] * o_ref[...]
    if final:
        y = y * lax.rsqrt(jnp.mean(y * y, axis=-1, keepdims=True) + RMS_EPS) * fg_ref[...]
    o_ref[...] = y


def _ffn(x, mod, g, final_g, w_in, w_o, *, row0, seq, tm, tf, final=False):
    T, D = x.shape
    tpb = seq // tm
    n_tiles = T // tm
    return pl.pallas_call(
        functools.partial(_ffn_kernel, row0=row0, final=final),
        grid=(n_tiles,),
        in_specs=[
            pl.BlockSpec((tm, D), lambda i: (i, 0)),
            pl.BlockSpec((tm, D), lambda i: (jnp.minimum(i + 1, n_tiles - 1), 0)),
            pl.BlockSpec((None, N_MOD, D), lambda i: (i // tpb, 0, 0)),
            pl.BlockSpec((None, N_MOD, D), lambda i: (jnp.minimum(i + 1, n_tiles - 1) // tpb, 0, 0)),
            pl.BlockSpec((1, D), lambda i: (0, 0)),
            pl.BlockSpec((1, D), lambda i: (0, 0)),
            pl.BlockSpec(memory_space=pl.ANY),
            pl.BlockSpec(memory_space=pl.ANY),
        ],
        out_specs=pl.BlockSpec((tm, D), lambda i: (i, 0)),
        out_shape=jax.ShapeDtypeStruct((T, D), F32),
        scratch_shapes=[
            pltpu.VMEM((tm, D), BF16),
            pltpu.VMEM((tm, D), BF16),
            pltpu.VMEM((FFN_SLOTS, 2, D, tf), BF16),
            pltpu.VMEM((FFN_SLOTS, tf, D), BF16),
            pltpu.SemaphoreType.DMA((FFN_SLOTS, 3)),
        ],
        compiler_params=_params(("arbitrary",)),
        name="ffn",
    )(x, x, mod, mod, g, final_g, w_in, w_o)


def _mix_in_kernel(x_ref, mod_ref, g_ref, w_ref, wvt_ref, cos_ref, sin_ref, pw_ref, pb_ref, ps_ref,
                   q_ref, k_ref, vt_ref, yp_ref, pext_ref, *, tpb, width, n_heads):
    i = pl.program_id(0)
    tm = x_ref.shape[0]
    h = _norm_modulate(x_ref[...], g_ref[...], mod_ref[3:4, :], mod_ref[4:5, :]).astype(BF16)

    p = jnp.dot(h, w_ref[:, 0:width], preferred_element_type=F32)

    @pl.when(i % tpb == 0)
    def _():
        pext_ref[0:POOL_HALO, :] = jnp.zeros((POOL_HALO, width), F32)

    pext_ref[POOL_HALO:POOL_HALO + tm, :] = p
    pos = (i % tpb) * tm + lax.broadcasted_iota(jnp.int32, (tm, 1), 0)
    cg = width // len(POOL_WINDOWS)
    for gi, win in enumerate(POOL_WINDOWS):
        lo, hi = gi * cg, (gi + 1) * cg
        pg = p[:, lo:hi]
        acc = pg
        for back in range(1, win):
            acc = acc + pext_ref[POOL_HALO - back:POOL_HALO - back + tm, lo:hi]
        count = jnp.minimum(pos + 1, win).astype(F32)
        pooled = acc / count - pg
        mixed = jnp.dot(pooled.astype(BF16), pw_ref[gi], preferred_element_type=F32) + pb_ref[:, lo:hi]
        yp_ref[:, lo:hi] = (mixed * ps_ref[:, lo:hi]).astype(BF16)
    pext_ref[0:POOL_HALO, :] = pext_ref[tm:tm + POOL_HALO, :]

    cos = cos_ref[...]
    sin = sin_ref[...]
    lane = lax.broadcasted_iota(jnp.int32, (tm, HEAD_V), 1)
    first_half = (lane % HEAD_DIM) < (HEAD_DIM // 2)
    for sec, out_ref, mul in ((1, q_ref, Q_SCALE), (2, k_ref, None)):
        z = jnp.dot(h, w_ref[:, sec * width:(sec + 1) * width], preferred_element_type=F32)
        for hh in range(n_heads):
            zh = z[:, hh * HEAD_V:(hh + 1) * HEAD_V]
            partner = jnp.where(first_half,
                                pltpu.roll(zh, HEAD_V - HEAD_DIM // 2, axis=1),
                                pltpu.roll(zh, HEAD_DIM // 2, axis=1))
            r = zh * cos + partner * sin
            if mul is not None:
                r = r * mul
            out_ref[hh] = r.astype(BF16)

    tk = vt_ref.shape[3]
    zt = lax.dot_general(wvt_ref[...], h, (((1,), (1,)), ((), ())), preferred_element_type=F32)
    ones_rows = (lax.broadcasted_iota(jnp.int32, (ONES_ROWS, tk), 0) == 0).astype(BF16)
    for hh in range(n_heads):
        for ck in range(tm // tk):
            vt_ref[hh, ck, 0:HEAD_V, :] = zt[hh * HEAD_V:(hh + 1) * HEAD_V, ck * tk:(ck + 1) * tk].astype(BF16)
            vt_ref[hh, ck, HEAD_V:VT_ROWS, :] = ones_rows


def _mix_in(x, mod, g, w_in, w_vt, cos, sin, pool_w, pool_b, pool_scale, *, seq, tm, tk):
    T, D = x.shape
    width = w_vt.shape[0]
    n_heads = width // HEAD_V
    tpb = seq // tm
    head_major = jax.ShapeDtypeStruct((n_heads, T, HEAD_V), BF16)
    head_spec = pl.BlockSpec((n_heads, tm, HEAD_V), lambda i: (0, i, 0))
    vt_shape = jax.ShapeDtypeStruct((n_heads, T // tk, VT_ROWS, tk), BF16)
    vt_spec = pl.BlockSpec((n_heads, tm // tk, VT_ROWS, tk), lambda i: (0, i, 0, 0))
    return pl.pallas_call(
        functools.partial(_mix_in_kernel, tpb=tpb, width=width, n_heads=n_heads),
        grid=(T // tm,),
        in_specs=[
            pl.BlockSpec((tm, D), lambda i: (i, 0)),
            pl.BlockSpec((None, N_MOD, D), lambda i: (i // tpb, 0, 0)),
            pl.BlockSpec((1, D), lambda i: (0, 0)),
            _resident((D, 3 * width), lambda i: (0, 0)),
            _resident((width, D), lambda i: (0, 0)),
            pl.BlockSpec((tm, HEAD_V), lambda i: (i % tpb, 0)),
            pl.BlockSpec((tm, HEAD_V), lambda i: (i % tpb, 0)),
            _resident(pool_w.shape, lambda i: (0, 0, 0)),
            pl.BlockSpec((1, width), lambda i: (0, 0)),
            pl.BlockSpec((1, width), lambda i: (0, 0)),
        ],
        out_specs=[head_spec, head_spec, vt_spec, pl.BlockSpec((tm, width), lambda i: (i, 0))],
        out_shape=[head_major, head_major, vt_shape, jax.ShapeDtypeStruct((T, width), BF16)],
        scratch_shapes=[pltpu.VMEM((POOL_HALO + tm, width), F32)],
        compiler_params=_params(("arbitrary",)),
        name="mix_in",
    )(x, mod, g, w_in, w_vt, cos, sin, pool_w, pool_b, pool_scale)


def _attn_kernel(linit_ref, q_ref, k_ref, vt_ref, dl_ref, g_ref, o_ref,
                 qs_ref, sa_ref, sb_ref, mca_ref, mcb_ref, m_ref, acc_ref, *, tk):
    i = pl.program_id(2)
    tq = q_ref.shape[0]
    vc = vt_ref.shape[2]

    q = q_ref[...]
    lane = lax.broadcasted_iota(jnp.int32, q.shape, 1)
    qs_ref[0:tq, :] = jnp.where(lane < HEAD_DIM, q, jnp.zeros_like(q))
    qs_ref[tq:2 * tq, :] = jnp.where(lane < HEAD_DIM, jnp.zeros_like(q), q)
    m_ref[...] = jnp.full(m_ref.shape, MASK_VALUE, F32)
    acc_ref[...] = jnp.zeros(acc_ref.shape, F32)

    last = (i * tq) // tk

    buf_a = (sa_ref, mca_ref)
    buf_b = (sb_ref, mcb_ref)

    sub = min(ATTN_SUB, vc)
    n_sub = tk // sub

    def score_sub(j, u, masked):
        start = pl.multiple_of(j * tk + u * sub, sub)
        s = lax.dot_general(k_ref[pl.ds(start, sub), :], qs_ref[...], (((1,), (1,)), ((), ())),
                            preferred_element_type=F32)
        if masked:
            key_pos = start + lax.broadcasted_iota(jnp.int32, s.shape, 0)
            col = lax.broadcasted_iota(jnp.int32, s.shape, 1)
            q_pos = i * tq + jnp.where(col >= tq, col - tq, col)
            s = jnp.where(key_pos <= q_pos, s, MASK_VALUE)
        return s

    def step(j, cur, nxt, next_masked):
        if cur is not None:
            s_cur, mc_cur = cur
            m_prev = m_ref[...]
            m_new = jnp.maximum(m_prev, mc_cur[...])
            alpha = jnp.exp2(m_prev - m_new)
        pv = None
        mx = None
        for u in range(n_sub):
            if nxt is not None:
                s = score_sub(j + 1 if cur is not None else j, u, next_masked)
                nxt[0][u * sub:(u + 1) * sub, :] = s
                smax = jnp.max(s, axis=0, keepdims=True)
                mx = smax if mx is None else jnp.maximum(mx, smax)
            if cur is not None:
                p = jnp.exp2(s_cur[u * sub:(u + 1) * sub, :] - m_new).astype(BF16)
                lo = (u * sub) % vc
                vt = vt_ref[j * (tk // vc) + (u * sub) // vc, :, lo:lo + sub]
                d = jnp.dot(vt, p, preferred_element_type=F32)
                pv = d if pv is None else pv + d
        if nxt is not None:
            nxt[1][...] = mx
        if cur is not None:
            acc_ref[...] = alpha * acc_ref[...] + pv
            m_ref[...] = m_new

    @pl.when(last == 0)
    def _():
        step(0, None, buf_a, True)

    @pl.when(last > 0)
    def _():
        step(0, None, buf_a, False)

    trips = jnp.maximum(last - 1, 0) // 2

    def body(t, carry):
        step(2 * t, buf_a, buf_b, False)
        step(2 * t + 1, buf_b, buf_a, False)
        return carry

    lax.fori_loop(0, trips, body, 0)
    done = 2 * trips

    @pl.when(last == done)
    def _():
        step(last, buf_a, None, None)

    @pl.when(last == done + 1)
    def _():
        step(done, buf_a, buf_b, True)
        step(last, buf_b, None, None)

    @pl.when(last == done + 2)
    def _():
        step(done, buf_a, buf_b, False)
        step(done + 1, buf_b, buf_a, True)
        step(last, buf_a, None, None)

    dl = dl_ref[...]
    lambda_init = linit_ref[0]
    lam = (jnp.exp(jnp.sum(dl[0:1] * dl[1:2], axis=1, keepdims=True))
           - jnp.exp(jnp.sum(dl[2:3] * dl[3:4], axis=1, keepdims=True)) + lambda_init)
    num = acc_ref[0:HEAD_V, :]
    den = acc_ref[HEAD_V:HEAD_V + 1, :]
    inv = 1.0 / den
    o = num[:, 0:tq] * inv[:, 0:tq] - lam * (num[:, tq:2 * tq] * inv[:, tq:2 * tq])
    o = o * lax.rsqrt(jnp.mean(o * o, axis=0, keepdims=True) + SUBLN_EPS)
    o_ref[...] = (o.T * g_ref[...] * (1.0 - lambda_init)).astype(BF16)


def _attention(linit, q, k, vt, diff_lambda, subln, *, batch, seq, tq, tk):
    n_heads, T, _ = q.shape
    vc = vt.shape[3]
    nq = seq // tq
    nk = seq // vc
    return pl.pallas_call(
        functools.partial(_attn_kernel, tk=tk),
        grid=(batch, n_heads, nq),
        in_specs=[
            pl.BlockSpec(memory_space=pltpu.SMEM),
            pl.BlockSpec((None, tq, HEAD_V), lambda b, h, i: (h, b * nq + i, 0)),
            pl.BlockSpec((None, seq, HEAD_V), lambda b, h, i: (h, b, 0)),
            pl.BlockSpec((None, nk, VT_ROWS, vc), lambda b, h, i: (h, b, 0, 0)),
            pl.BlockSpec(diff_lambda.shape, lambda b, h, i: (0, 0)),
            pl.BlockSpec((1, HEAD_V), lambda b, h, i: (0, 0)),
        ],
        out_specs=pl.BlockSpec((tq, HEAD_V), lambda b, h, i: (b * nq + i, h)),
        out_shape=jax.ShapeDtypeStruct((T, n_heads * HEAD_V), BF16),
        scratch_shapes=[
            pltpu.VMEM((2 * tq, HEAD_V), BF16),
            pltpu.VMEM((tk, 2 * tq), F32),
            pltpu.VMEM((tk, 2 * tq), F32),
            pltpu.VMEM((1, 2 * tq), F32),
            pltpu.VMEM((1, 2 * tq), F32),
            pltpu.VMEM((1, 2 * tq), F32),
            pltpu.VMEM((VT_ROWS, 2 * tq), F32),
        ],
        compiler_params=_params(("parallel", "parallel", "arbitrary")),
        name="diff_attn",
    )(linit, q, k, vt, diff_lambda, subln)


def _mix_out_kernel(x_ref, mod_ref, yp_ref, ya_ref, w_ref, o_ref):
    width = yp_ref.shape[1]
    y = jnp.dot(yp_ref[...], w_ref[0:width, :], preferred_element_type=F32)
    y = y + jnp.dot(ya_ref[...], w_ref[width:2 * width, :], preferred_element_type=F32)
    o_ref[...] = x_ref[...] + mod_ref[5:6, :] * y


def _mix_out(x, mod, y_pool, y_attn, w_out, *, seq, tm):
    T, D = x.shape
    width = y_pool.shape[1]
    tpb = seq // tm
    return pl.pallas_call(
        _mix_out_kernel,
        grid=(T // tm,),
        in_specs=[
            pl.BlockSpec((tm, D), lambda i: (i, 0)),
            pl.BlockSpec((None, N_MOD, D), lambda i: (i // tpb, 0, 0)),
            pl.BlockSpec((tm, width), lambda i: (i, 0)),
            pl.BlockSpec((tm, width), lambda i: (i, 0)),
            _resident(w_out.shape, lambda i: (0, 0)),
        ],
        out_specs=pl.BlockSpec((tm, D), lambda i: (i, 0)),
        out_shape=jax.ShapeDtypeStruct((T, D), F32),
        compiler_params=_params(("parallel",)),
        name="mix_out",
    )(x, mod, y_pool, y_attn, w_out)


def _rope_tables(seq):
    pos = jnp.arange(seq, dtype=F32)
    inv_freq = ROPE_THETA ** (-jnp.arange(0, HEAD_DIM, 2, dtype=F32) / HEAD_DIM)
    ang = pos[:, None] * inv_freq[None, :]
    c, s = jnp.cos(ang), jnp.sin(ang)
    return jnp.concatenate([c, c, c, c], axis=1), jnp.concatenate([-s, s, -s, s], axis=1)


def kernel(x, c, w_mod, b_mod, norm_ffn1, ffn1_w_in, ffn1_w_out, norm_mix, w_in, pool_w, pool_b, pool_scale,
           diff_lambda, diff_subln, w_out, norm_ffn2, ffn2_w_in, ffn2_w_out, final_norm):
    B, S, D = x.shape
    L = w_mod.shape[0]
    F = ffn1_w_out.shape[1]
    tm = min(TOKEN_TILE, S)
    tq = min(Q_TILE, S)
    tk = min(KV_CHUNK, S)
    vc = min(VT_CHUNK, tk, tm)
    tf = FF_CHUNK if F % FF_CHUNK == 0 else F
    assert S % tm == 0 and S % tk == 0 and tm % vc == 0 and tk % vc == 0 and tk % tq == 0
    width = pool_scale.shape[1]

    cos, sin = _rope_tables(S)
    final_g = final_norm.reshape(1, D)
    mod_all = _modulation(c, w_mod, b_mod).reshape(L, B, N_MOD, D)
    xt = x.reshape(B * S, D)
    for l in range(L):
        mod = mod_all[l]
        linit = jnp.full((1,), 0.8 - 0.6 * math.exp(-0.3 * l), F32)
        xt = _ffn(xt, mod, norm_ffn1[l].reshape(1, D), final_g, ffn1_w_in[l].astype(BF16),
                  ffn1_w_out[l].astype(BF16), row0=0, seq=S, tm=tm, tf=tf)
        w_pqk = w_in[l, :, :3 * width].astype(BF16)
        w_vt = w_in[l, :, 3 * width:].T.astype(BF16)
        q, k, vt, y_pool = _mix_in(xt, mod, norm_mix[l].reshape(1, D), w_pqk, w_vt, cos, sin,
                                   pool_w[l].astype(BF16), pool_b[l].reshape(1, width),
                                   pool_scale[l].reshape(1, width), seq=S, tm=tm, tk=vc)
        y_attn = _attention(linit, q, k, vt, diff_lambda[l], diff_subln[l].reshape(1, HEAD_V),
                            batch=B, seq=S, tq=tq, tk=tk)
        xt = _mix_out(xt, mod, y_pool, y_attn, w_out[l].astype(BF16), seq=S, tm=tm)
        xt = _ffn(xt, mod, norm_ffn2[l].reshape(1, D), final_g, ffn2_w_in[l].astype(BF16),
                  ffn2_w_out[l].astype(BF16), row0=6, seq=S, tm=tm, tf=tf, final=(l == L - 1))
    return xt.reshape(B, S, D)
```
